```python
import jax, jax.numpy as jnp
from jax import lax
import numpy as np

D_MODEL = 1024
BATCH = 16
SEQ = 2048
DEPTH = 1

D_MIX = D_MODEL
A_HEADS = 8
A_HEAD_DIM = 64
A_WIDTH = A_HEADS * A_HEAD_DIM
IDX_HEADS = 8
IDX_DIM = 32
TOPK_MAX = 256
B_HEADS = 8
B_NOPE_DIM = 64
B_ROPE_DIM = 32
B_V_DIM = 64
B_WIDTH = B_HEADS * B_V_DIM
Q_LORA = 256
KV_LORA = 128

ROPE_THETA = 10000.0
Q_BLOCK = 128
DEEPNORM_ALPHA = (2 * DEPTH) ** 0.25
DEEPNORM_BETA = (8 * DEPTH) ** -0.25
LN_EPS = 1e-5
RMS_EPS = 1e-6

IN_SPLITS = (A_WIDTH, A_WIDTH, A_WIDTH, A_WIDTH, IDX_HEADS * IDX_DIM, IDX_DIM, IDX_HEADS,
             Q_LORA, KV_LORA, B_ROPE_DIM, B_WIDTH)
D_IN = sum(IN_SPLITS)

kernel_name = "hymba_dsa_mla_deepnorm_layer"


def _split_cols(h):
    offs = np.cumsum(np.array(IN_SPLITS))[:-1].tolist()
    return jnp.split(h, offs, axis=-1)


def _rope(x, pos):
    d = x.shape[-1]
    inv = ROPE_THETA ** (-jnp.arange(0, d, 2, dtype=jnp.float32) / d)
    ang = pos.astype(jnp.float32)[..., None] * inv
    cos = jnp.cos(ang)[:, :, None, :]
    sin = jnp.sin(ang)[:, :, None, :]
    xf = x.astype(jnp.float32)
    x1, x2 = xf[..., : d // 2], xf[..., d // 2:]
    out = jnp.concatenate([x1 * cos - x2 * sin, x1 * sin + x2 * cos], axis=-1)
    return out.astype(x.dtype)


def _rmsnorm(x, g):
    xf = x.astype(jnp.float32)
    y = xf * lax.rsqrt(jnp.mean(xf * xf, axis=-1, keepdims=True) + RMS_EPS)
    return (y * g.astype(jnp.float32)).astype(x.dtype)


def _layernorm(x, g, b):
    xf = x.astype(jnp.float32)
    mu = jnp.mean(xf, axis=-1, keepdims=True)
    var = jnp.mean(jnp.square(xf - mu), axis=-1, keepdims=True)
    y = (xf - mu) * lax.rsqrt(var + LN_EPS)
    return (y * g.astype(jnp.float32) + b.astype(jnp.float32)).astype(x.dtype)


def _dsa_branch(q, k, v, iq, ik, iw):
    B, S, H, Dh = q.shape
    k_top = min(TOPK_MAX, S // 4)
    n_blocks = S // Q_BLOCK
    key_pos = jnp.arange(S)
    scale = A_HEAD_DIM ** -0.5
    idx_scale = IDX_DIM ** -0.5
    ikf = ik.astype(jnp.float32)
    iwf = iw.astype(jnp.float32) * (IDX_HEADS ** -0.5)

    def block(i):
        t0 = i * Q_BLOCK
        qb = lax.dynamic_slice_in_dim(q, t0, Q_BLOCK, axis=1).astype(jnp.float32)
        iqb = lax.dynamic_slice_in_dim(iq, t0, Q_BLOCK, axis=1).astype(jnp.float32)
        iwb = lax.dynamic_slice_in_dim(iwf, t0, Q_BLOCK, axis=1)
        qpos = t0 + jnp.arange(Q_BLOCK)
        logits = jnp.einsum('bthd,bsd->bths', iqb, ikf) * idx_scale
        score = jnp.einsum('bth,bths->bts', iwb, jax.nn.relu(logits))
        causal = key_pos[None, :] <= qpos[:, None]
        score = jnp.where(causal[None], score, -jnp.inf)
        _, sel = lax.top_k(score, k_top)
        ks = jax.vmap(lambda kb, ib: kb[ib])(k, sel).astype(jnp.float32)
        vs = jax.vmap(lambda vb, ib: vb[ib])(v, sel).astype(jnp.float32)
        att = jnp.einsum('bthd,btkhd->bhtk', qb, ks) * scale
        valid = sel <= qpos[None, :, None]
        att = jnp.where(valid[:, None], att, -jnp.inf)
        p = jax.nn.softmax(att, axis=-1)
        o = jnp.einsum('bhtk,btkhd->bthd', p, vs)
        return o.astype(q.dtype)

    out = lax.map(block, jnp.arange(n_blocks))
    return out.transpose(1, 0, 2, 3, 4).reshape(B, S, H * Dh)


def _mla_branch(c_q, c_kv, k_rope, q_norm_g, w_uq, kv_norm_g, w_ukv, pos):
    B, S, _ = c_q.shape
    q = (_rmsnorm(c_q, q_norm_g) @ w_uq).reshape(B, S, B_HEADS, B_NOPE_DIM + B_ROPE_DIM)
    q_nope, q_pe = q[..., :B_NOPE_DIM], _rope(q[..., B_NOPE_DIM:], pos)
    kv = (_rmsnorm(c_kv, kv_norm_g) @ w_ukv).reshape(B, S, B_HEADS, B_NOPE_DIM + B_V_DIM)
    k_nope = kv[..., :B_NOPE_DIM].astype(jnp.float32)
    v = kv[..., B_NOPE_DIM:].astype(jnp.float32)
    k_pe = _rope(k_rope[:, :, None, :], pos)[:, :, 0, :].astype(jnp.float32)
    scale = (B_NOPE_DIM + B_ROPE_DIM) ** -0.5
    n_blocks = S // Q_BLOCK
    key_pos = jnp.arange(S)

    def block(i):
        t0 = i * Q_BLOCK
        qn = lax.dynamic_slice_in_dim(q_nope, t0, Q_BLOCK, axis=1).astype(jnp.float32)
        qp = lax.dynamic_slice_in_dim(q_pe, t0, Q_BLOCK, axis=1).astype(jnp.float32)
        qpos = t0 + jnp.arange(Q_BLOCK)
        s = (jnp.einsum('bthd,bshd->bhts', qn, k_nope)
             + jnp.einsum('bthr,bsr->bhts', qp, k_pe)) * scale
        causal = key_pos[None, :] <= qpos[:, None]
        s = jnp.where(causal[None, None], s, -jnp.inf)
        p = jax.nn.softmax(s, axis=-1)
        o = jnp.einsum('bhts,bshd->bthd', p, v)
        return o.astype(c_q.dtype)

    out = lax.map(block, jnp.arange(n_blocks))
    return out.transpose(1, 0, 2, 3, 4).reshape(B, S, B_WIDTH)


def setup_inputs(seed: int = 0) -> dict:
    key = jax.random.key(seed)
    ks = jax.random.split(key, 10)
    x = jax.random.normal(ks[0], (BATCH, SEQ, D_MODEL), jnp.float32)
    positions = jnp.broadcast_to(jnp.arange(SEQ, dtype=jnp.int32), (BATCH, SEQ))
    col_scale = jnp.concatenate([
        jnp.ones((2 * A_WIDTH,), jnp.float32),
        jnp.full((A_WIDTH,), DEEPNORM_BETA, jnp.float32),
        jnp.ones((D_IN - 3 * A_WIDTH,), jnp.float32)])
    w_in = jax.random.normal(ks[1], (DEPTH, D_MODEL, D_IN), jnp.float32) * (D_MODEL ** -0.5) * col_scale
    q_norm_g = 1.0 + 0.01 * jax.random.normal(ks[2], (DEPTH, Q_LORA), jnp.float32)
    w_uq = jax.random.normal(ks[3], (DEPTH, Q_LORA, B_HEADS * (B_NOPE_DIM + B_ROPE_DIM)), jnp.float32) * (Q_LORA ** -0.5)
    kv_norm_g = 1.0 + 0.01 * jax.random.normal(ks[4], (DEPTH, KV_LORA), jnp.float32)
    ukv_scale = jnp.concatenate([jnp.ones((B_NOPE_DIM,), jnp.float32),
                                 jnp.full((B_V_DIM,), DEEPNORM_BETA, jnp.float32)])
    w_ukv = (jax.random.normal(ks[5], (DEPTH, KV_LORA, B_HEADS, B_NOPE_DIM + B_V_DIM), jnp.float32)
             * (KV_LORA ** -0.5) * ukv_scale).reshape(DEPTH, KV_LORA, B_HEADS * (B_NOPE_DIM + B_V_DIM))
    w_out = jax.random.normal(ks[6], (DEPTH, D_MIX, D_MODEL), jnp.float32) * (D_MIX ** -0.5) * DEEPNORM_BETA
    ln_g = 1.0 + 0.01 * jax.random.normal(ks[7], (DEPTH, D_MODEL), jnp.float32)
    ln_b = 0.01 * jax.random.normal(ks[8], (DEPTH, D_MODEL), jnp.float32)
    return {"x": x, "positions": positions, "w_in": w_in, "q_norm_g": q_norm_g,
            "w_uq": w_uq, "kv_norm_g": kv_norm_g, "w_ukv": w_ukv, "w_out": w_out,
            "ln_g": ln_g, "ln_b": ln_b}


def reference(x, positions, w_in, q_norm_g, w_uq, kv_norm_g, w_ukv, w_out, ln_g, ln_b):
    B, S, _ = x.shape
    for l in range(DEPTH):
        h = x @ w_in[l]
        (qa, ka, va, ga, iq, ik, iw, c_q, c_kv, k_rope, gb) = _split_cols(h)
        qa = _rope(qa.reshape(B, S, A_HEADS, A_HEAD_DIM), positions)
        ka = _rope(ka.reshape(B, S, A_HEADS, A_HEAD_DIM), positions)
        va = va.reshape(B, S, A_HEADS, A_HEAD_DIM)
        iq = _rope(iq.reshape(B, S, IDX_HEADS, IDX_DIM), positions)
        ik = _rope(ik[:, :, None, :], positions)[:, :, 0, :]
        o_a = _dsa_branch(qa, ka, va, iq, ik, iw) * jax.nn.silu(ga)
        o_b = _mla_branch(c_q, c_kv, k_rope, q_norm_g[l], w_uq[l], kv_norm_g[l],
                          w_ukv[l], positions) * jax.nn.silu(gb)
        out = jnp.concatenate([o_a, o_b], axis=-1) @ w_out[l]
        x = _layernorm(DEEPNORM_ALPHA * x + out, ln_g[l], ln_b[l])
    return x
```

```python
import functools

import jax
import jax.numpy as jnp
import numpy as np
from jax import lax
from jax.experimental import pallas as pl
from jax.experimental.pallas import tpu as pltpu

F32 = jnp.float32
BF16 = jnp.bfloat16
I32 = jnp.int32

D_MODEL = 1024
A_HEADS = 8
A_HEAD_DIM = 64
A_WIDTH = A_HEADS * A_HEAD_DIM
IDX_HEADS = 8
IDX_DIM = 32
TOPK_MAX = 256
B_HEADS = 8
B_NOPE_DIM = 64
B_ROPE_DIM = 32
B_V_DIM = 64
B_WIDTH = B_HEADS * B_V_DIM
Q_LORA = 256
KV_LORA = 128
ROPE_THETA = 10000.0
LN_EPS = 1e-5
RMS_EPS = 1e-6

IN_SPLITS = (A_WIDTH, A_WIDTH, A_WIDTH, A_WIDTH, IDX_HEADS * IDX_DIM, IDX_DIM, IDX_HEADS,
             Q_LORA, KV_LORA, B_ROPE_DIM, B_WIDTH)

LANES = 128
CHUNK = 256
ROW_TILE = 512
VMEM_LIMIT = 56 * 1024 * 1024
NEG = -1e30
INT_MIN = -2 ** 31

NT_DIMS = (((1,), (1,)), ((), ()))
TN_DIMS = (((0,), (0,)), ((), ()))


def _nt(a, b):
    return lax.dot_general(a, b, NT_DIMS, preferred_element_type=F32)


def _rope_table_kernel(pos_ref, f64_ref, sg64_ref, f32_ref, sg32_ref,
                       c64_ref, s64_ref, c32_ref, s32_ref):
    pos = pos_ref[...].astype(F32)
    a64 = pos * f64_ref[...]
    c64_ref[...] = jnp.cos(a64)
    s64_ref[...] = jnp.sin(a64) * sg64_ref[...]
    a32 = pos * f32_ref[...]
    c32_ref[...] = jnp.cos(a32)
    s32_ref[...] = jnp.sin(a32) * sg32_ref[...]


def _rope_tables(pos_col):
    n = pos_col.shape[0]
    tm = ROW_TILE
    lane = np.arange(LANES)

    def pattern(d):
        inv = ROPE_THETA ** (-jnp.arange(0, d, 2, dtype=F32) / d)
        freq = inv[lane % (d // 2)][None, :]
        sign = jnp.asarray(np.where((lane % d) < d // 2, -1.0, 1.0), F32)[None, :]
        return freq, sign

    f64, sg64 = pattern(A_HEAD_DIM)
    f32_, sg32 = pattern(IDX_DIM)
    row = pl.BlockSpec((1, LANES), lambda t: (0, 0))
    tab = pl.BlockSpec((tm, LANES), lambda t: (t, 0))
    return pl.pallas_call(
        _rope_table_kernel,
        grid=(n // tm,),
        in_specs=[pl.BlockSpec((tm, 1), lambda t: (t, 0)), row, row, row, row],
        out_specs=[tab, tab, tab, tab],
        out_shape=[jax.ShapeDtypeStruct((n, LANES), F32)] * 4,
        compiler_params=pltpu.CompilerParams(dimension_semantics=("parallel",)),
        name="rope_tables",
    )(pos_col, f64, sg64, f32_, sg32)


def _rope_group(xg, cos, sin_signed, low, half):
    swapped = jnp.where(low, pltpu.roll(xg, LANES - half, 1), pltpu.roll(xg, half, 1))
    return xg * cos + swapped * sin_signed


def _proj_kernel(x_ref, c64_ref, s64_ref, c32_ref, s32_ref,
                 wq_ref, wk_ref, wiq_ref, wik_ref, wkpe_ref, wcq_ref, wckv_ref,
                 wvT_ref, wgT_ref, wiwT_ref, gq_ref, gkv_ref, wuq_ref, wukvk_ref, wukvvT_ref,
                 qa_ref, ka_ref, iq_ref, ik_ref, qb_ref, kb_ref,
                 vaT_ref, vbT_ref, gT_ref, iwT_ref):
    tm = x_ref.shape[0]
    xb = x_ref[...].astype(BF16)
    c64, s64 = c64_ref[...], s64_ref[...]
    c32, s32 = c32_ref[...], s32_ref[...]
    lane = lax.broadcasted_iota(I32, (tm, LANES), 1)
    low64 = (lane % A_HEAD_DIM) < A_HEAD_DIM // 2
    low32 = (lane % IDX_DIM) < IDX_DIM // 2

    def dot(w_ref):
        return jnp.dot(xb, w_ref[...], preferred_element_type=F32)

    def rope_all(h, cos, sin, low, half, out_ref, scale=None):
        for g in range(h.shape[1] // LANES):
            r = _rope_group(h[:, g * LANES:(g + 1) * LANES], cos, sin, low, half)
            if scale is not None:
                r = r * scale
            out_ref[:, g * LANES:(g + 1) * LANES] = r.astype(out_ref.dtype)

    rope_all(dot(wq_ref), c64, s64, low64, A_HEAD_DIM // 2, qa_ref, A_HEAD_DIM ** -0.5)
    rope_all(dot(wk_ref), c64, s64, low64, A_HEAD_DIM // 2, ka_ref)
    rope_all(dot(wiq_ref), c32, s32, low32, IDX_DIM // 2, iq_ref)
    rope_all(dot(wik_ref), c32, s32, low32, IDX_DIM // 2, ik_ref)

    def rmsnorm(c, g_ref):
        ms = jnp.mean(c * c, axis=-1, keepdims=True)
        return (c * lax.rsqrt(ms + RMS_EPS) * g_ref[...]).astype(BF16)

    cqn = rmsnorm(dot(wcq_ref), gq_ref)
    qb = jnp.dot(cqn, wuq_ref[...], preferred_element_type=F32)
    pe_lane = (lane >= B_NOPE_DIM) & (lane < B_NOPE_DIM + B_ROPE_DIM)
    b_scale = (B_NOPE_DIM + B_ROPE_DIM) ** -0.5
    for h in range(B_HEADS):
        g = qb[:, h * LANES:(h + 1) * LANES]
        r = jnp.where(pe_lane, _rope_group(g, c32, s32, low32, B_ROPE_DIM // 2), g)
        qb_ref[:, h * LANES:(h + 1) * LANES] = (r * b_scale).astype(BF16)

    ckvn = rmsnorm(dot(wckv_ref), gkv_ref)
    kpe = _rope_group(dot(wkpe_ref), c32, s32, low32, B_ROPE_DIM // 2)
    kn = jnp.dot(ckvn, wukvk_ref[...], preferred_element_type=F32)
    for h in range(B_HEADS):
        kb_ref[:, h * LANES:(h + 1) * LANES] = (kn[:, h * LANES:(h + 1) * LANES] + kpe).astype(BF16)

    def store_t(val, out_ref):
        for j in range(tm // CHUNK):
            out_ref[j] = val[:, j * CHUNK:(j + 1) * CHUNK].astype(out_ref.dtype)

    store_t(_nt(wukvvT_ref[...], ckvn), vbT_ref)
    store_t(_nt(wvT_ref[...], xb), vaT_ref)
    gt = _nt(wgT_ref[...], xb)
    store_t(gt * jax.nn.sigmoid(gt), gT_ref)
    store_t(_nt(wiwT_ref[...], xb) * (IDX_DIM ** -0.5 * IDX_HEADS ** -0.5), iwT_ref)


def _projection(x2, tables, weights):
    n = x2.shape[0]
    tm = ROW_TILE
    nchunks = n // CHUNK
    cpt = tm // CHUNK

    def rows(width):
        return pl.BlockSpec((tm, width), lambda t: (t, 0))

    def full(a):
        return pl.BlockSpec(a.shape, lambda t: (0,) * a.ndim)

    def tspec(feat):
        return pl.BlockSpec((cpt, feat, CHUNK), lambda t: (t, 0, 0))

    out_shape = [
        jax.ShapeDtypeStruct((n, A_WIDTH), BF16),
        jax.ShapeDtypeStruct((n, A_WIDTH), BF16),
        jax.ShapeDtypeStruct((n, IDX_HEADS * IDX_DIM), BF16),
        jax.ShapeDtypeStruct((n, LANES), BF16),
        jax.ShapeDtypeStruct((n, B_HEADS * LANES), BF16),
        jax.ShapeDtypeStruct((n, B_HEADS * LANES), BF16),
        jax.ShapeDtypeStruct((nchunks, A_WIDTH, CHUNK), BF16),
        jax.ShapeDtypeStruct((nchunks, B_WIDTH, CHUNK), BF16),
        jax.ShapeDtypeStruct((nchunks, A_WIDTH + B_WIDTH, CHUNK), F32),
        jax.ShapeDtypeStruct((nchunks, 16, CHUNK), F32),
    ]
    out_specs = [rows(A_WIDTH), rows(A_WIDTH), rows(IDX_HEADS * IDX_DIM), rows(LANES),
                 rows(B_HEADS * LANES), rows(B_HEADS * LANES),
                 tspec(A_WIDTH), tspec(B_WIDTH), tspec(A_WIDTH + B_WIDTH), tspec(16)]
    in_specs = [rows(D_MODEL)] + [rows(LANES)] * 4 + [full(w) for w in weights]
    return pl.pallas_call(
        _proj_kernel,
        grid=(n // tm,),
        in_specs=in_specs,
        out_specs=out_specs,
        out_shape=out_shape,
        compiler_params=pltpu.CompilerParams(dimension_semantics=("parallel",),
                                             vmem_limit_bytes=VMEM_LIMIT),
        name="projection",
    )(x2, *tables, *weights)


def _dsa_kernel(iq_ref, iwT_ref, ik_ref, qa_ref, ka_ref, vaT_ref, o_ref,
                iqm_ref, key_ref, bias_ref, *, k_top):
    C = CHUNK
    i = pl.program_id(1)
    nk = i + 1
    lane = lax.broadcasted_iota(I32, (C, LANES), 1)
    row = lax.broadcasted_iota(I32, (C, C), 0)
    col = lax.broadcasted_iota(I32, (C, C), 1)

    for h in range(IDX_HEADS):
        g = iq_ref[:, (h // 4) * LANES:(h // 4 + 1) * LANES]
        iqm_ref[h] = jnp.where((lane // IDX_DIM) == (h % 4), g, jnp.zeros_like(g))
    w = iwT_ref[...]

    def chunk_rows(c):
        return pl.ds(pl.multiple_of(c * C, C), C)

    def score_chunk(c, carry):
        ikc = ik_ref[chunk_rows(c), :]
        acc = jnp.zeros((C, C), F32)
        for h in range(IDX_HEADS):
            acc = acc + w[h:h + 1, :] * jnp.maximum(_nt(ikc, iqm_ref[h]), 0.0)
        acc = acc + 0.0
        bits = lax.bitcast_convert_type(acc, I32)
        key = jnp.where(bits < 0, bits ^ jnp.int32(0x7FFFFFFF), bits)
        valid = (c * C + row) <= (i * C + col)
        key_ref[chunk_rows(c), :] = jnp.where(valid, key, jnp.int32(INT_MIN))
        return carry

    lax.fori_loop(0, nk, score_chunk, 0)

    def count(pred):
        def body(c, acc8):
            hit = pred(key_ref[chunk_rows(c), :]).astype(I32)
            return acc8 + jnp.sum(hit.reshape(C // 8, 8, C), axis=0)
        acc8 = lax.fori_loop(0, nk, body, jnp.zeros((8, C), I32))
        return jnp.sum(acc8, axis=0, keepdims=True)

    def bit_body(b, v):
        cand = v + jnp.left_shift(jnp.int32(1), 31 - b)
        return jnp.where(count(lambda k: k >= cand) >= k_top, cand, v)

    thr = lax.fori_loop(0, 32, bit_body, jnp.full((1, C), INT_MIN, I32))

    need = (k_top - count(lambda k: k > thr)).astype(F32)
    tri = (row >= col).astype(BF16)

    def bias_chunk(c, carry):
        key = key_ref[chunk_rows(c), :]
        eq = key == thr
        prefix = jnp.dot(tri, eq.astype(BF16), preferred_element_type=F32) + carry
        sel = (key > thr) | (eq & (prefix <= need))
        valid = (c * C + row) <= (i * C + col)
        bias_ref[chunk_rows(c), :] = jnp.where(sel & valid, 0.0, NEG)
        return carry + jnp.sum(eq.astype(F32), axis=0, keepdims=True)

    lax.fori_loop(0, nk, bias_chunk, jnp.zeros((1, C), F32))

    for h in range(A_HEADS):
        pair = h // 2
        qp = qa_ref[:, pair * LANES:(pair + 1) * LANES]
        qh = jnp.where((lane // A_HEAD_DIM) == (h % 2), qp, jnp.zeros_like(qp))

        def att_chunk(c, carry, qh=qh, pair=pair, h=h):
            m, l, acc = carry
            kc = ka_ref[chunk_rows(c), pair * LANES:(pair + 1) * LANES]
            s = _nt(kc, qh) + bias_ref[chunk_rows(c), :]
            m_new = jnp.maximum(m, jnp.max(s, axis=0, keepdims=True))
            p = jnp.exp(s - m_new)
            alpha = jnp.exp(m - m_new)
            l = alpha * l + jnp.sum(p, axis=0, keepdims=True)
            vt = vaT_ref[c, h * A_HEAD_DIM:(h + 1) * A_HEAD_DIM, :]
            acc = alpha * acc + jnp.dot(vt, p.astype(BF16), preferred_element_type=F32)
            return m_new, l, acc

        init = (jnp.full((1, C), NEG, F32), jnp.zeros((1, C), F32), jnp.zeros((A_HEAD_DIM, C), F32))
        m, l, acc = lax.fori_loop(0, nk, att_chunk, init)
        o_ref[h * A_HEAD_DIM:(h + 1) * A_HEAD_DIM, :] = (acc * (1.0 / l)).astype(o_ref.dtype)


def _dsa_attention(iq, iwT, ik, qa, ka, vaT, batch, seq):
    C = CHUNK
    nc = seq // C
    k_top = min(TOPK_MAX, seq // 4)
    return pl.pallas_call(
        functools.partial(_dsa_kernel, k_top=k_top),
        grid=(batch, nc),
        in_specs=[
            pl.BlockSpec((C, IDX_HEADS * IDX_DIM), lambda b, i: (b * nc + i, 0)),
            pl.BlockSpec((None, 16, C), lambda b, i: (b * nc + i, 0, 0)),
            pl.BlockSpec((seq, LANES), lambda b, i: (b, 0)),
            pl.BlockSpec((C, A_WIDTH), lambda b, i: (b * nc + i, 0)),
            pl.BlockSpec((seq, A_WIDTH), lambda b, i: (b, 0)),
            pl.BlockSpec((nc, A_WIDTH, C), lambda b, i: (b, 0, 0)),
        ],
        out_specs=pl.BlockSpec((None, A_WIDTH, C), lambda b, i: (b * nc + i, 0, 0)),
        out_shape=jax.ShapeDtypeStruct((batch * nc, A_WIDTH, C), F32),
        scratch_shapes=[
            pltpu.VMEM((IDX_HEADS, C, LANES), BF16),
            pltpu.VMEM((seq, C), I32),
            pltpu.VMEM((seq, C), F32),
        ],
        compiler_params=pltpu.CompilerParams(dimension_semantics=("parallel", "arbitrary"),
                                             vmem_limit_bytes=VMEM_LIMIT),
        name="dsa_attention",
    )(iq, iwT, ik, qa, ka, vaT)


def _mla_kernel(qb_ref, kb_ref, vbT_ref, o_ref):
    C = CHUNK
    i = pl.program_id(1)
    row = lax.broadcasted_iota(I32, (C, C), 0)
    col = lax.broadcasted_iota(I32, (C, C), 1)
    causal = row <= col

    for h in range(B_HEADS):
        qh = qb_ref[:, h * LANES:(h + 1) * LANES]

        def att_chunk(c, carry, diagonal, qh=qh, h=h):
            m, l, acc = carry
            kc = kb_ref[pl.ds(pl.multiple_of(c * C, C), C), h * LANES:(h + 1) * LANES]
            s = _nt(kc, qh)
            if diagonal:
                s = jnp.where(causal, s, NEG)
            m_new = jnp.maximum(m, jnp.max(s, axis=0, keepdims=True))
            p = jnp.exp(s - m_new)
            alpha = jnp.exp(m - m_new)
            l = alpha * l + jnp.sum(p, axis=0, keepdims=True)
            vt = vbT_ref[c, h * B_V_DIM:(h + 1) * B_V_DIM, :]
            acc = alpha * acc + jnp.dot(vt, p.astype(BF16), preferred_element_type=F32)
            return m_new, l, acc

        init = (jnp.full((1, C), NEG, F32), jnp.zeros((1, C), F32), jnp.zeros((B_V_DIM, C), F32))
        carry = lax.fori_loop(0, i, functools.partial(att_chunk, diagonal=False), init)
        m, l, acc = att_chunk(i, carry, diagonal=True)
        o_ref[h * B_V_DIM:(h + 1) * B_V_DIM, :] = (acc * (1.0 / l)).astype(o_ref.dtype)


def _mla_attention(qb, kb, vbT, batch, seq):
    C = CHUNK
    nc = seq // C
    return pl.pallas_call(
        _mla_kernel,
        grid=(batch, nc),
        in_specs=[
            pl.BlockSpec((C, B_HEADS * LANES), lambda b, i: (b * nc + i, 0)),
            pl.BlockSpec((seq, B_HEADS * LANES), lambda b, i: (b, 0)),
            pl.BlockSpec((nc, B_WIDTH, C), lambda b, i: (b, 0, 0)),
        ],
        out_specs=pl.BlockSpec((None, B_WIDTH, C), lambda b, i: (b * nc + i, 0, 0)),
        out_shape=jax.ShapeDtypeStruct((batch * nc, B_WIDTH, C), F32),
        compiler_params=pltpu.CompilerParams(dimension_semantics=("parallel", "arbitrary"),
                                             vmem_limit_bytes=VMEM_LIMIT),
        name="mla_attention",
    )(qb, kb, vbT)


def _out_kernel(oa_ref, ob_ref, g_ref, x_ref, w_ref, lng_ref, lnb_ref, y_ref, *, alpha):
    ga = (oa_ref[...] * g_ref[:A_WIDTH, :]).astype(BF16)
    gb = (ob_ref[...] * g_ref[A_WIDTH:, :]).astype(BF16)
    out = (lax.dot_general(ga, w_ref[:A_WIDTH, :], TN_DIMS, preferred_element_type=F32)
           + lax.dot_general(gb, w_ref[A_WIDTH:, :], TN_DIMS, preferred_element_type=F32))
    z = alpha * x_ref[...] + out
    mu = jnp.mean(z, axis=-1, keepdims=True)
    zc = z - mu
    var = jnp.mean(zc * zc, axis=-1, keepdims=True)
    y_ref[...] = zc * lax.rsqrt(var + LN_EPS) * lng_ref[...] + lnb_ref[...]


def _output(oaT, obT, gT, x2, w_out, ln_g, ln_b, alpha):
    C = CHUNK
    n = x2.shape[0]
    const = lambda a: pl.BlockSpec(a.shape, lambda t: (0,) * a.ndim)
    return pl.pallas_call(
        functools.partial(_out_kernel, alpha=alpha),
        grid=(n // C,),
        in_specs=[
            pl.BlockSpec((None, A_WIDTH, C), lambda t: (t, 0, 0)),
            pl.BlockSpec((None, B_WIDTH, C), lambda t: (t, 0, 0)),
            pl.BlockSpec((None, A_WIDTH + B_WIDTH, C), lambda t: (t, 0, 0)),
            pl.BlockSpec((C, D_MODEL), lambda t: (t, 0)),
            const(w_out), const(ln_g), const(ln_b),
        ],
        out_specs=pl.BlockSpec((C, D_MODEL), lambda t: (t, 0)),
        out_shape=jax.ShapeDtypeStruct((n, D_MODEL), F32),
        compiler_params=pltpu.CompilerParams(dimension_semantics=("parallel",),
                                             vmem_limit_bytes=VMEM_LIMIT),
        name="output",
    )(oaT, obT, gT, x2, w_out, ln_g, ln_b)


def _prepare_weights(w_in, q_norm_g, w_uq, kv_norm_g, w_ukv):
    offs = np.concatenate([[0], np.cumsum(IN_SPLITS)])
    (wqa, wka, wva, wga, wiq, wik, wiw, wcq, wckv, wkr, wgb) = [
        w_in[:, int(offs[j]):int(offs[j + 1])] for j in range(len(IN_SPLITS))]
    d = w_in.shape[0]
    zeros = lambda c: jnp.zeros((d, c), w_in.dtype)
    wik4 = jnp.concatenate([wik] * (LANES // IDX_DIM), axis=1)
    wkpe = jnp.concatenate([zeros(B_NOPE_DIM), wkr, zeros(LANES - B_NOPE_DIM - B_ROPE_DIM)], axis=1)
    wiwT = jnp.concatenate([wiw.T, jnp.zeros((16 - IDX_HEADS, d), w_in.dtype)], axis=0)
    wgT = jnp.concatenate([wga, wgb], axis=1).T

    qk = B_NOPE_DIM + B_ROPE_DIM
    wuq = w_uq.reshape(Q_LORA, B_HEADS, qk)
    wuq = jnp.pad(wuq, ((0, 0), (0, 0), (0, LANES - qk))).reshape(Q_LORA, B_HEADS * LANES)
    wukv = w_ukv.reshape(KV_LORA, B_HEADS, B_NOPE_DIM + B_V_DIM)
    wukvk = jnp.pad(wukv[:, :, :B_NOPE_DIM], ((0, 0), (0, 0), (0, LANES - B_NOPE_DIM)))
    wukvk = wukvk.reshape(KV_LORA, B_HEADS * LANES)
    wukvvT = wukv[:, :, B_NOPE_DIM:].reshape(KV_LORA, B_WIDTH).T

    bf = lambda a: a.astype(BF16)
    return [bf(wqa), bf(wka), bf(wiq), bf(wik4), bf(wkpe), bf(wcq), bf(wckv),
            bf(wva.T), bf(wgT), bf(wiwT),
            q_norm_g.reshape(1, Q_LORA).astype(F32), kv_norm_g.reshape(1, KV_LORA).astype(F32),
            bf(wuq), bf(wukvk), bf(wukvvT)]


def kernel(x, positions, w_in, q_norm_g, w_uq, kv_norm_g, w_ukv, w_out, ln_g, ln_b):
    batch, seq, d_model = x.shape
    depth = w_in.shape[0]
    alpha = (2 * depth) ** 0.25
    assert d_model == D_MODEL and seq % ROW_TILE == 0 and ROW_TILE % CHUNK == 0
    n = batch * seq
    tables = _rope_tables(positions.reshape(n, 1).astype(I32))
    x2 = x.reshape(n, d_model)
    for l in range(depth):
        weights = _prepare_weights(w_in[l], q_norm_g[l], w_uq[l], kv_norm_g[l], w_ukv[l])
        qa, ka, iq, ik, qb, kb, vaT, vbT, gT, iwT = _projection(x2, tables, weights)
        oaT = _dsa_attention(iq, iwT, ik, qa, ka, vaT, batch, seq)
        obT = _mla_attention(qb, kb, vbT, batch, seq)
        x2 = _output(oaT, obT, gT, x2, w_out[l].astype(BF16),
                     ln_g[l].reshape(1, d_model), ln_b[l].reshape(1, d_model), alpha)
    return x2.reshape(batch, seq, d_model)
```

```python
import functools

import jax
import jax.numpy as jnp
import numpy as np
from jax import lax
from jax.experimental import pallas as pl
from jax.experimental.pallas import tpu as pltpu

F32 = jnp.float32
BF16 = jnp.bfloat16
I32 = jnp.int32

D_MODEL = 1024
A_HEADS = 8
A_HEAD_DIM = 64
A_WIDTH = A_HEADS * A_HEAD_DIM
IDX_HEADS = 8
IDX_DIM = 32
TOPK_MAX = 256
B_HEADS = 8
B_NOPE_DIM = 64
B_ROPE_DIM = 32
B_V_DIM = 64
B_WIDTH = B_HEADS * B_V_DIM
Q_LORA = 256
KV_LORA = 128
ROPE_THETA = 10000.0
LN_EPS = 1e-5
RMS_EPS = 1e-6

IN_SPLITS = (A_WIDTH, A_WIDTH, A_WIDTH, A_WIDTH, IDX_HEADS * IDX_DIM, IDX_DIM, IDX_HEADS,
             Q_LORA, KV_LORA, B_ROPE_DIM, B_WIDTH)

LANES = 128
CHUNK = 256
ROW_TILE = 512
VMEM_LIMIT = 56 * 1024 * 1024
NEG = -1e30
INT_MIN = -2 ** 31
LOG2E = 1.4426950408889634
L_ROWS = 16
LOOKAHEAD = 4

NT_DIMS = (((1,), (1,)), ((), ()))
TN_DIMS = (((0,), (0,)), ((), ()))


def _nt(a, b):
    return lax.dot_general(a, b, NT_DIMS, preferred_element_type=F32)


def _rope_table_kernel(pos_ref, f64_ref, sg64_ref, f32_ref, sg32_ref,
                       c64_ref, s64_ref, c32_ref, s32_ref):
    pos = pos_ref[...].astype(F32)
    a64 = pos * f64_ref[...]
    c64_ref[...] = jnp.cos(a64)
    s64_ref[...] = jnp.sin(a64) * sg64_ref[...]
    a32 = pos * f32_ref[...]
    c32_ref[...] = jnp.cos(a32)
    s32_ref[...] = jnp.sin(a32) * sg32_ref[...]


def _rope_tables(pos_col):
    n = pos_col.shape[0]
    tm = ROW_TILE
    lane = np.arange(LANES)

    def pattern(d):
        inv = ROPE_THETA ** (-jnp.arange(0, d, 2, dtype=F32) / d)
        freq = inv[lane % (d // 2)][None, :]
        sign = jnp.asarray(np.where((lane % d) < d // 2, -1.0, 1.0), F32)[None, :]
        return freq, sign

    f64, sg64 = pattern(A_HEAD_DIM)
    f32_, sg32 = pattern(IDX_DIM)
    row = pl.BlockSpec((1, LANES), lambda t: (0, 0))
    tab = pl.BlockSpec((tm, LANES), lambda t: (t, 0))
    return pl.pallas_call(
        _rope_table_kernel,
        grid=(n // tm,),
        in_specs=[pl.BlockSpec((tm, 1), lambda t: (t, 0)), row, row, row, row],
        out_specs=[tab, tab, tab, tab],
        out_shape=[jax.ShapeDtypeStruct((n, LANES), F32)] * 4,
        compiler_params=pltpu.CompilerParams(dimension_semantics=("parallel",)),
        name="rope_tables",
    )(pos_col, f64, sg64, f32_, sg32)


def _rope_group(xg, cos, sin_signed, low, half):
    swapped = jnp.where(low, pltpu.roll(xg, LANES - half, 1), pltpu.roll(xg, half, 1))
    return xg * cos + swapped * sin_signed


def _proj_kernel(x_ref, c64_ref, s64_ref, c32_ref, s32_ref,
                 wq_ref, wk_ref, wiq_ref, wik_ref, wkpe_ref, wcq_ref, wckv_ref,
                 wvT_ref, wgT_ref, wiwT_ref, gq_ref, gkv_ref, wuq_ref, wukvk_ref, wukvvT_ref,
                 qa_ref, ka_ref, iq_ref, ik_ref, qb_ref, kb_ref,
                 vaT_ref, vbT_ref, gT_ref, iwT_ref):
    tm = x_ref.shape[0]
    xb = x_ref[...].astype(BF16)
    c64, s64 = c64_ref[...], s64_ref[...]
    c32, s32 = c32_ref[...], s32_ref[...]
    lane = lax.broadcasted_iota(I32, (tm, LANES), 1)
    low64 = (lane % A_HEAD_DIM) < A_HEAD_DIM // 2
    low32 = (lane % IDX_DIM) < IDX_DIM // 2

    def dot(w_ref):
        return jnp.dot(xb, w_ref[...], preferred_element_type=F32)

    def rope_all(h, cos, sin, low, half, out_ref, scale=None, head_dim=None):
        for g in range(h.shape[1] // LANES):
            r = _rope_group(h[:, g * LANES:(g + 1) * LANES], cos, sin, low, half)
            if scale is not None:
                r = r * scale
            if head_dim is None:
                out_ref[:, g * LANES:(g + 1) * LANES] = r.astype(out_ref.dtype)
                continue
            per_group = LANES // head_dim
            for j in range(per_group):
                o = (g * per_group + j) * LANES
                out_ref[:, o:o + LANES] = jnp.where(lane // head_dim == j, r, 0.0).astype(out_ref.dtype)

    rope_all(dot(wq_ref), c64, s64, low64, A_HEAD_DIM // 2, qa_ref, A_HEAD_DIM ** -0.5 * LOG2E,
             head_dim=A_HEAD_DIM)
    rope_all(dot(wk_ref), c64, s64, low64, A_HEAD_DIM // 2, ka_ref)
    rope_all(dot(wiq_ref), c32, s32, low32, IDX_DIM // 2, iq_ref, head_dim=IDX_DIM)
    rope_all(dot(wik_ref), c32, s32, low32, IDX_DIM // 2, ik_ref)

    def rmsnorm(c, g_ref):
        ms = jnp.mean(c * c, axis=-1, keepdims=True)
        return (c * lax.rsqrt(ms + RMS_EPS) * g_ref[...]).astype(BF16)

    cqn = rmsnorm(dot(wcq_ref), gq_ref)
    qb = jnp.dot(cqn, wuq_ref[...], preferred_element_type=F32)
    pe_lane = (lane >= B_NOPE_DIM) & (lane < B_NOPE_DIM + B_ROPE_DIM)
    b_scale = (B_NOPE_DIM + B_ROPE_DIM) ** -0.5 * LOG2E
    for h in range(B_HEADS):
        g = qb[:, h * LANES:(h + 1) * LANES]
        r = jnp.where(pe_lane, _rope_group(g, c32, s32, low32, B_ROPE_DIM // 2), g)
        qb_ref[:, h * LANES:(h + 1) * LANES] = (r * b_scale).astype(BF16)

    ckvn = rmsnorm(dot(wckv_ref), gkv_ref)
    kpe = _rope_group(dot(wkpe_ref), c32, s32, low32, B_ROPE_DIM // 2)
    kn = jnp.dot(ckvn, wukvk_ref[...], preferred_element_type=F32)
    for h in range(B_HEADS):
        kb_ref[:, h * LANES:(h + 1) * LANES] = (kn[:, h * LANES:(h + 1) * LANES] + kpe).astype(BF16)

    def store_t(val, out_ref):
        for j in range(tm // CHUNK):
            out_ref[j] = val[:, j * CHUNK:(j + 1) * CHUNK].astype(out_ref.dtype)

    store_t(_nt(wukvvT_ref[...], ckvn), vbT_ref)
    store_t(_nt(wvT_ref[...], xb), vaT_ref)
    gt = _nt(wgT_ref[...], xb)
    store_t(gt * jax.nn.sigmoid(gt), gT_ref)
    store_t(_nt(wiwT_ref[...], xb) * (IDX_DIM ** -0.5 * IDX_HEADS ** -0.5), iwT_ref)


def _projection(x2, tables, weights):
    n = x2.shape[0]
    tm = ROW_TILE
    nchunks = n // CHUNK
    cpt = tm // CHUNK

    def rows(width):
        return pl.BlockSpec((tm, width), lambda t: (t, 0))

    def full(a):
        return pl.BlockSpec(a.shape, lambda t: (0,) * a.ndim)

    def tspec(feat):
        return pl.BlockSpec((cpt, feat, CHUNK), lambda t: (t, 0, 0))

    out_shape = [
        jax.ShapeDtypeStruct((n, A_HEADS * LANES), BF16),
        jax.ShapeDtypeStruct((n, A_WIDTH), BF16),
        jax.ShapeDtypeStruct((n, IDX_HEADS * LANES), BF16),
        jax.ShapeDtypeStruct((n, LANES), BF16),
        jax.ShapeDtypeStruct((n, B_HEADS * LANES), BF16),
        jax.ShapeDtypeStruct((n, B_HEADS * LANES), BF16),
        jax.ShapeDtypeStruct((nchunks, A_WIDTH, CHUNK), BF16),
        jax.ShapeDtypeStruct((nchunks, B_WIDTH, CHUNK), BF16),
        jax.ShapeDtypeStruct((nchunks, A_WIDTH + B_WIDTH, CHUNK), F32),
        jax.ShapeDtypeStruct((nchunks, 16, CHUNK), F32),
    ]
    out_specs = [rows(A_HEADS * LANES), rows(A_WIDTH), rows(IDX_HEADS * LANES), rows(LANES),
                 rows(B_HEADS * LANES), rows(B_HEADS * LANES),
                 tspec(A_WIDTH), tspec(B_WIDTH), tspec(A_WIDTH + B_WIDTH), tspec(16)]
    in_specs = [rows(D_MODEL)] + [rows(LANES)] * 4 + [full(w) for w in weights]
    return pl.pallas_call(
        _proj_kernel,
        grid=(n // tm,),
        in_specs=in_specs,
        out_specs=out_specs,
        out_shape=out_shape,
        compiler_params=pltpu.CompilerParams(dimension_semantics=("parallel",),
                                             vmem_limit_bytes=VMEM_LIMIT),
        name="projection",
    )(x2, *tables, *weights)


def _softmax_step(s, vt, m_ref, acc_ref, h):
    m_old = m_ref[h]
    m_new = jnp.maximum(m_old, jnp.max(s, axis=0, keepdims=True))
    p = jnp.exp2(s - m_new).astype(BF16)
    vt_ones = jnp.concatenate([vt, jnp.ones((L_ROWS, vt.shape[1]), BF16)], axis=0)
    pv = jnp.dot(vt_ones, p, preferred_element_type=F32)
    acc_ref[h] = jnp.exp2(m_old - m_new) * acc_ref[h] + pv
    m_ref[h] = m_new


def _pipelined_heads(heads, logits, consume):
    pending = [logits(h) for h in range(min(LOOKAHEAD, heads))]
    for h in range(heads):
        s = pending.pop(0)
        if h + LOOKAHEAD < heads:
            pending.append(logits(h + LOOKAHEAD))
        consume(h, s)


def _softmax_init(m_ref, acc_ref):
    m_ref[...] = jnp.full(m_ref.shape, NEG, F32)
    acc_ref[...] = jnp.zeros(acc_ref.shape, F32)


def _softmax_finish(acc_ref, o_ref, heads, dv):
    for h in range(heads):
        acc = acc_ref[h]
        o_ref[h * dv:(h + 1) * dv, :] = (acc[:dv] * (1.0 / acc[dv:dv + 1])).astype(o_ref.dtype)


def _dsa_kernel(iq_ref, iwT_ref, ik_ref, qa_ref, ka_ref, vaT_ref, o_ref,
                key_ref, bias_ref, m_ref, acc_ref, *, k_top):
    C = CHUNK
    i = pl.program_id(1)
    nk = i + 1
    row = lax.broadcasted_iota(I32, (C, C), 0)
    col = lax.broadcasted_iota(I32, (C, C), 1)
    w = iwT_ref[...]

    def chunk_rows(c):
        return pl.ds(pl.multiple_of(c * C, C), C)

    def score_chunk(c, carry):
        ikc = ik_ref[chunk_rows(c), :]
        acc = jnp.zeros((C, C), F32)
        for h in range(IDX_HEADS):
            lg = _nt(ikc, iq_ref[:, h * LANES:(h + 1) * LANES])
            acc = acc + w[h:h + 1, :] * jnp.maximum(lg, 0.0)
        acc = acc + 0.0
        bits = lax.bitcast_convert_type(acc, I32)
        key = jnp.where(bits < 0, bits ^ jnp.int32(0x7FFFFFFF), bits)
        valid = (c * C + row) <= (i * C + col)
        key_ref[chunk_rows(c), :] = jnp.where(valid, key, jnp.int32(INT_MIN))
        return carry

    lax.fori_loop(0, nk, score_chunk, 0)

    def count(pred):
        def body(c, acc8):
            hit = pred(key_ref[chunk_rows(c), :]).astype(I32)
            return acc8 + jnp.sum(hit.reshape(C // 8, 8, C), axis=0)
        acc8 = lax.fori_loop(0, nk, body, jnp.zeros((8, C), I32))
        return jnp.sum(acc8, axis=0, keepdims=True)

    def bit_body(b, v):
        cand = v + jnp.left_shift(jnp.int32(1), 31 - b)
        return jnp.where(count(lambda k: k >= cand) >= k_top, cand, v)

    thr = lax.fori_loop(0, 32, bit_body, jnp.full((1, C), INT_MIN, I32))

    need = (k_top - count(lambda k: k > thr)).astype(F32)
    tri = (row >= col).astype(BF16)

    def bias_chunk(c, carry):
        key = key_ref[chunk_rows(c), :]
        eq = key == thr
        prefix = jnp.dot(tri, eq.astype(BF16), preferred_element_type=F32) + carry
        sel = (key > thr) | (eq & (prefix <= need))
        valid = (c * C + row) <= (i * C + col)
        bias_ref[chunk_rows(c), :] = jnp.where(sel & valid, 0.0, NEG)
        return carry + jnp.sum(eq.astype(F32), axis=0, keepdims=True)

    lax.fori_loop(0, nk, bias_chunk, jnp.zeros((1, C), F32))

    _softmax_init(m_ref, acc_ref)

    def att_chunk(c, carry):
        def logits(h):
            s = _nt(ka_ref[chunk_rows(c), (h // 2) * LANES:(h // 2 + 1) * LANES],
                    qa_ref[:, h * LANES:(h + 1) * LANES])
            return s + bias_ref[chunk_rows(c), :]

        def consume(h, s):
            vt = vaT_ref[c, h * A_HEAD_DIM:(h + 1) * A_HEAD_DIM, :]
            _softmax_step(s, vt, m_ref, acc_ref, h)

        _pipelined_heads(A_HEADS, logits, consume)
        return carry

    lax.fori_loop(0, nk, att_chunk, 0)
    _softmax_finish(acc_ref, o_ref, A_HEADS, A_HEAD_DIM)


def _dsa_attention(iq, iwT, ik, qa, ka, vaT, batch, seq):
    C = CHUNK
    nc = seq // C
    k_top = min(TOPK_MAX, seq // 4)
    return pl.pallas_call(
        functools.partial(_dsa_kernel, k_top=k_top),
        grid=(batch, nc),
        in_specs=[
            pl.BlockSpec((C, IDX_HEADS * LANES), lambda b, i: (b * nc + i, 0)),
            pl.BlockSpec((None, 16, C), lambda b, i: (b * nc + i, 0, 0)),
            pl.BlockSpec((seq, LANES), lambda b, i: (b, 0)),
            pl.BlockSpec((C, A_HEADS * LANES), lambda b, i: (b * nc + i, 0)),
            pl.BlockSpec((seq, A_WIDTH), lambda b, i: (b, 0)),
            pl.BlockSpec((nc, A_WIDTH, C), lambda b, i: (b, 0, 0)),
        ],
        out_specs=pl.BlockSpec((None, A_WIDTH, C), lambda b, i: (b * nc + i, 0, 0)),
        out_shape=jax.ShapeDtypeStruct((batch * nc, A_WIDTH, C), F32),
        scratch_shapes=[
            pltpu.VMEM((seq, C), I32),
            pltpu.VMEM((seq, C), F32),
            pltpu.VMEM((A_HEADS, 1, C), F32),
            pltpu.VMEM((A_HEADS, A_HEAD_DIM + L_ROWS, C), F32),
        ],
        compiler_params=pltpu.CompilerParams(dimension_semantics=("parallel", "arbitrary"),
                                             vmem_limit_bytes=VMEM_LIMIT),
        name="dsa_attention",
    )(iq, iwT, ik, qa, ka, vaT)


def _mla_kernel(qb_ref, kb_ref, vbT_ref, o_ref, m_ref, acc_ref):
    C = CHUNK
    i = pl.program_id(1)
    row = lax.broadcasted_iota(I32, (C, C), 0)
    col = lax.broadcasted_iota(I32, (C, C), 1)
    causal = row <= col
    _softmax_init(m_ref, acc_ref)

    def att_chunk(c, carry, diagonal):
        def logits(h):
            kc = kb_ref[pl.ds(pl.multiple_of(c * C, C), C), h * LANES:(h + 1) * LANES]
            return _nt(kc, qb_ref[:, h * LANES:(h + 1) * LANES])

        def consume(h, s):
            if diagonal:
                s = jnp.where(causal, s, NEG)
            _softmax_step(s, vbT_ref[c, h * B_V_DIM:(h + 1) * B_V_DIM, :], m_ref, acc_ref, h)

        _pipelined_heads(B_HEADS, logits, consume)
        return carry

    lax.fori_loop(0, i, functools.partial(att_chunk, diagonal=False), 0)
    att_chunk(i, 0, diagonal=True)
    _softmax_finish(acc_ref, o_ref, B_HEADS, B_V_DIM)


def _mla_attention(qb, kb, vbT, batch, seq):
    C = CHUNK
    nc = seq // C
    return pl.pallas_call(
        _mla_kernel,
        grid=(batch, nc),
        in_specs=[
            pl.BlockSpec((C, B_HEADS * LANES), lambda b, i: (b * nc + i, 0)),
            pl.BlockSpec((seq, B_HEADS * LANES), lambda b, i: (b, 0)),
            pl.BlockSpec((nc, B_WIDTH, C), lambda b, i: (b, 0, 0)),
        ],
        out_specs=pl.BlockSpec((None, B_WIDTH, C), lambda b, i: (b * nc + i, 0, 0)),
        out_shape=jax.ShapeDtypeStruct((batch * nc, B_WIDTH, C), F32),
        scratch_shapes=[
            pltpu.VMEM((B_HEADS, 1, C), F32),
            pltpu.VMEM((B_HEADS, B_V_DIM + L_ROWS, C), F32),
        ],
        compiler_params=pltpu.CompilerParams(dimension_semantics=("parallel", "arbitrary"),
                                             vmem_limit_bytes=VMEM_LIMIT),
        name="mla_attention",
    )(qb, kb, vbT)


def _out_kernel(oa_ref, ob_ref, g_ref, x_ref, w_ref, lng_ref, lnb_ref, y_ref, *, alpha):
    ga = (oa_ref[...] * g_ref[:A_WIDTH, :]).astype(BF16)
    gb = (ob_ref[...] * g_ref[A_WIDTH:, :]).astype(BF16)
    out = (lax.dot_general(ga, w_ref[:A_WIDTH, :], TN_DIMS, preferred_element_type=F32)
           + lax.dot_general(gb, w_ref[A_WIDTH:, :], TN_DIMS, preferred_element_type=F32))
    z = alpha * x_ref[...] + out
    mu = jnp.mean(z, axis=-1, keepdims=True)
    zc = z - mu
    var = jnp.mean(zc * zc, axis=-1, keepdims=True)
    y_ref[...] = zc * lax.rsqrt(var + LN_EPS) * lng_ref[...] + lnb_ref[...]


def _output(oaT, obT, gT, x2, w_out, ln_g, ln_b, alpha):
    C = CHUNK
    n = x2.shape[0]
    const = lambda a: pl.BlockSpec(a.shape, lambda t: (0,) * a.ndim)
    return pl.pallas_call(
        functools.partial(_out_kernel, alpha=alpha),
        grid=(n // C,),
        in_specs=[
            pl.BlockSpec((None, A_WIDTH, C), lambda t: (t, 0, 0)),
            pl.BlockSpec((None, B_WIDTH, C), lambda t: (t, 0, 0)),
            pl.BlockSpec((None, A_WIDTH + B_WIDTH, C), lambda t: (t, 0, 0)),
            pl.BlockSpec((C, D_MODEL), lambda t: (t, 0)),
            const(w_out), const(ln_g), const(ln_b),
        ],
        out_specs=pl.BlockSpec((C, D_MODEL), lambda t: (t, 0)),
        out_shape=jax.ShapeDtypeStruct((n, D_MODEL), F32),
        compiler_params=pltpu.CompilerParams(dimension_semantics=("parallel",),
                                             vmem_limit_bytes=VMEM_LIMIT),
        name="output",
    )(oaT, obT, gT, x2, w_out, ln_g, ln_b)


def _prepare_weights(w_in, q_norm_g, w_uq, kv_norm_g, w_ukv):
    offs = np.concatenate([[0], np.cumsum(IN_SPLITS)])
    (wqa, wka, wva, wga, wiq, wik, wiw, wcq, wckv, wkr, wgb) = [
        w_in[:, int(offs[j]):int(offs[j + 1])] for j in range(len(IN_SPLITS))]
    d = w_in.shape[0]
    zeros = lambda c: jnp.zeros((d, c), w_in.dtype)
    wik4 = jnp.concatenate([wik] * (LANES // IDX_DIM), axis=1)
    wkpe = jnp.concatenate([zeros(B_NOPE_DIM), wkr, zeros(LANES - B_NOPE_DIM - B_ROPE_DIM)], axis=1)
    wiwT = jnp.concatenate([wiw.T, jnp.zeros((16 - IDX_HEADS, d), w_in.dtype)], axis=0)
    wgT = jnp.concatenate([wga, wgb], axis=1).T

    qk = B_NOPE_DIM + B_ROPE_DIM
    wuq = w_uq.reshape(Q_LORA, B_HEADS, qk)
    wuq = jnp.pad(wuq, ((0, 0), (0, 0), (0, LANES - qk))).reshape(Q_LORA, B_HEADS * LANES)
    wukv = w_ukv.reshape(KV_LORA, B_HEADS, B_NOPE_DIM + B_V_DIM)
    wukvk = jnp.pad(wukv[:, :, :B_NOPE_DIM], ((0, 0), (0, 0), (0, LANES - B_NOPE_DIM)))
    wukvk = wukvk.reshape(KV_LORA, B_HEADS * LANES)
    wukvvT = wukv[:, :, B_NOPE_DIM:].reshape(KV_LORA, B_WIDTH).T

    bf = lambda a: a.astype(BF16)
    return [bf(wqa), bf(wka), bf(wiq), bf(wik4), bf(wkpe), bf(wcq), bf(wckv),
            bf(wva.T), bf(wgT), bf(wiwT),
            q_norm_g.reshape(1, Q_LORA).astype(F32), kv_norm_g.reshape(1, KV_LORA).astype(F32),
            bf(wuq), bf(wukvk), bf(wukvvT)]


def kernel(x, positions, w_in, q_norm_g, w_uq, kv_norm_g, w_ukv, w_out, ln_g, ln_b):
    batch, seq, d_model = x.shape
    depth = w_in.shape[0]
    alpha = (2 * depth) ** 0.25
    assert d_model == D_MODEL and seq % ROW_TILE == 0 and ROW_TILE % CHUNK == 0
    n = batch * seq
    tables = _rope_tables(positions.reshape(n, 1).astype(I32))
    x2 = x.reshape(n, d_model)
    for l in range(depth):
        weights = _prepare_weights(w_in[l], q_norm_g[l], w_uq[l], kv_norm_g[l], w_ukv[l])
        qa, ka, iq, ik, qb, kb, vaT, vbT, gT, iwT = _projection(x2, tables, weights)
        oaT = _dsa_attention(iq, iwT, ik, qa, ka, vaT, batch, seq)
        obT = _mla_attention(qb, kb, vbT, batch, seq)
        x2 = _output(oaT, obT, gT, x2, w_out[l].astype(BF16),
                     ln_g[l].reshape(1, d_model), ln_b[l].reshape(1, d_model), alpha)
    return x2.reshape(batch, seq, d_model)
```

```python
import functools

import jax
import jax.numpy as jnp
import numpy as np
from jax import lax
from jax.experimental import pallas as pl
from jax.experimental.pallas import tpu as pltpu

F32 = jnp.float32
BF16 = jnp.bfloat16
I32 = jnp.int32

D_MODEL = 1024
A_HEADS = 8
A_HEAD_DIM = 64
A_WIDTH = A_HEADS * A_HEAD_DIM
IDX_HEADS = 8
IDX_DIM = 32
TOPK_MAX = 256
B_HEADS = 8
B_NOPE_DIM = 64
B_ROPE_DIM = 32
B_V_DIM = 64
B_WIDTH = B_HEADS * B_V_DIM
Q_LORA = 256
KV_LORA = 128
ROPE_THETA = 10000.0
LN_EPS = 1e-5
RMS_EPS = 1e-6

IN_SPLITS = (A_WIDTH, A_WIDTH, A_WIDTH, A_WIDTH, IDX_HEADS * IDX_DIM, IDX_DIM, IDX_HEADS,
             Q_LORA, KV_LORA, B_ROPE_DIM, B_WIDTH)

LANES = 128
CHUNK = 256
ROW_TILE = 512
VMEM_LIMIT = 56 * 1024 * 1024
NEG = -1e30
INT_MIN = -2 ** 31
LOG2E = 1.4426950408889634
L_ROWS = 16
LOOKAHEAD = 4

NT_DIMS = (((1,), (1,)), ((), ()))
TN_DIMS = (((0,), (0,)), ((), ()))


def _nt(a, b):
    return lax.dot_general(a, b, NT_DIMS, preferred_element_type=F32)


def _rope_patterns():
    lane = np.arange(LANES)
    rows = []
    for d in (A_HEAD_DIM, IDX_DIM):
        inv = ROPE_THETA ** (-jnp.arange(0, d, 2, dtype=F32) / d)
        rows.append(inv[lane % (d // 2)])
        rows.append(jnp.asarray(np.where((lane % d) < d // 2, -1.0, 1.0), F32))
    return jnp.stack(rows)


def _rope_tables(pos_ref, pat_ref):
    pos = pos_ref[...].astype(F32)
    a64 = pos * pat_ref[0:1, :]
    a32 = pos * pat_ref[2:3, :]
    return (jnp.cos(a64), jnp.sin(a64) * pat_ref[1:2, :],
            jnp.cos(a32), jnp.sin(a32) * pat_ref[3:4, :])


def _rope_group(xg, cos, sin_signed, low, half):
    swapped = jnp.where(low, pltpu.roll(xg, LANES - half, 1), pltpu.roll(xg, half, 1))
    return xg * cos + swapped * sin_signed


def _proj_kernel(x_ref, pos_ref, pat_ref,
                 wq_ref, wk_ref, wiq_ref, wik_ref, wkpe_ref, wcq_ref, wckv_ref,
                 wvT_ref, wgT_ref, wiwT_ref, gq_ref, gkv_ref, wuq_ref, wukvk_ref, wukvvT_ref,
                 qa_ref, ka_ref, iq_ref, ik_ref, qb_ref, kb_ref,
                 vaT_ref, vbT_ref, gT_ref, iwT_ref):
    tm = x_ref.shape[0]
    xb = x_ref[...].astype(BF16)
    c64, s64, c32, s32 = _rope_tables(pos_ref, pat_ref)
    lane = lax.broadcasted_iota(I32, (tm, LANES), 1)
    low64 = (lane % A_HEAD_DIM) < A_HEAD_DIM // 2
    low32 = (lane % IDX_DIM) < IDX_DIM // 2

    def dot(w_ref):
        return jnp.dot(xb, w_ref[...], preferred_element_type=F32)

    def rope_all(h, cos, sin, low, half, out_ref, scale=None, head_dim=None):
        for g in range(h.shape[1] // LANES):
            r = _rope_group(h[:, g * LANES:(g + 1) * LANES], cos, sin, low, half)
            if scale is not None:
                r = r * scale
            if head_dim is None:
                out_ref[:, g * LANES:(g + 1) * LANES] = r.astype(out_ref.dtype)
                continue
            per_group = LANES // head_dim
            for j in range(per_group):
                o = (g * per_group + j) * LANES
                out_ref[:, o:o + LANES] = jnp.where(lane // head_dim == j, r, 0.0).astype(out_ref.dtype)

    rope_all(dot(wq_ref), c64, s64, low64, A_HEAD_DIM // 2, qa_ref, A_HEAD_DIM ** -0.5 * LOG2E,
             head_dim=A_HEAD_DIM)
    rope_all(dot(wk_ref), c64, s64, low64, A_HEAD_DIM // 2, ka_ref)
    rope_all(dot(wiq_ref), c32, s32, low32, IDX_DIM // 2, iq_ref, head_dim=IDX_DIM)
    rope_all(dot(wik_ref), c32, s32, low32, IDX_DIM // 2, ik_ref)

    def rmsnorm(c, g_ref):
        ms = jnp.mean(c * c, axis=-1, keepdims=True)
        return (c * lax.rsqrt(ms + RMS_EPS) * g_ref[...]).astype(BF16)

    cqn = rmsnorm(dot(wcq_ref), gq_ref)
    qb = jnp.dot(cqn, wuq_ref[...], preferred_element_type=F32)
    pe_lane = (lane >= B_NOPE_DIM) & (lane < B_NOPE_DIM + B_ROPE_DIM)
    b_scale = (B_NOPE_DIM + B_ROPE_DIM) ** -0.5 * LOG2E
    for h in range(B_HEADS):
        g = qb[:, h * LANES:(h + 1) * LANES]
        r = jnp.where(pe_lane, _rope_group(g, c32, s32, low32, B_ROPE_DIM // 2), g)
        qb_ref[:, h * LANES:(h + 1) * LANES] = (r * b_scale).astype(BF16)

    ckvn = rmsnorm(dot(wckv_ref), gkv_ref)
    kpe = _rope_group(dot(wkpe_ref), c32, s32, low32, B_ROPE_DIM // 2)
    kn = jnp.dot(ckvn, wukvk_ref[...], preferred_element_type=F32)
    for h in range(B_HEADS):
        kb_ref[:, h * LANES:(h + 1) * LANES] = (kn[:, h * LANES:(h + 1) * LANES] + kpe).astype(BF16)

    def store_t(val, out_ref):
        for j in range(tm // CHUNK):
            out_ref[j] = val[:, j * CHUNK:(j + 1) * CHUNK].astype(out_ref.dtype)

    store_t(_nt(wukvvT_ref[...], ckvn), vbT_ref)
    store_t(_nt(wvT_ref[...], xb), vaT_ref)
    gt = _nt(wgT_ref[...], xb)
    store_t(gt * jax.nn.sigmoid(gt), gT_ref)
    store_t(_nt(wiwT_ref[...], xb) * (IDX_DIM ** -0.5 * IDX_HEADS ** -0.5), iwT_ref)


def _projection(x2, pos_col, weights):
    n = x2.shape[0]
    tm = ROW_TILE
    nchunks = n // CHUNK
    cpt = tm // CHUNK

    def rows(width):
        return pl.BlockSpec((tm, width), lambda t: (t, 0))

    def full(a):
        return pl.BlockSpec(a.shape, lambda t: (0,) * a.ndim)

    def tspec(feat):
        return pl.BlockSpec((cpt, feat, CHUNK), lambda t: (t, 0, 0))

    out_shape = [
        jax.ShapeDtypeStruct((n, A_HEADS * LANES), BF16),
        jax.ShapeDtypeStruct((n, A_WIDTH), BF16),
        jax.ShapeDtypeStruct((n, IDX_HEADS * LANES), BF16),
        jax.ShapeDtypeStruct((n, LANES), BF16),
        jax.ShapeDtypeStruct((n, B_HEADS * LANES), BF16),
        jax.ShapeDtypeStruct((n, B_HEADS * LANES), BF16),
        jax.ShapeDtypeStruct((nchunks, A_WIDTH, CHUNK), BF16),
        jax.ShapeDtypeStruct((nchunks, B_WIDTH, CHUNK), BF16),
        jax.ShapeDtypeStruct((nchunks, A_WIDTH + B_WIDTH, CHUNK), F32),
        jax.ShapeDtypeStruct((nchunks, 16, CHUNK), F32),
    ]
    out_specs = [rows(A_HEADS * LANES), rows(A_WIDTH), rows(IDX_HEADS * LANES), rows(LANES),
                 rows(B_HEADS * LANES), rows(B_HEADS * LANES),
                 tspec(A_WIDTH), tspec(B_WIDTH), tspec(A_WIDTH + B_WIDTH), tspec(16)]
    patterns = _rope_patterns()
    in_specs = [rows(D_MODEL), rows(1), full(patterns)] + [full(w) for w in weights]
    return pl.pallas_call(
        _proj_kernel,
        grid=(n // tm,),
        in_specs=in_specs,
        out_specs=out_specs,
        out_shape=out_shape,
        compiler_params=pltpu.CompilerParams(dimension_semantics=("parallel",),
                                             vmem_limit_bytes=VMEM_LIMIT),
        name="projection",
    )(x2, pos_col, patterns, *weights)


def _softmax_step(s, vt, m_ref, acc_ref, h):
    m_old = m_ref[h]
    m_new = jnp.maximum(m_old, jnp.max(s, axis=0, keepdims=True))
    p = jnp.exp2(s - m_new).astype(BF16)
    vt_ones = jnp.concatenate([vt, jnp.ones((L_ROWS, vt.shape[1]), BF16)], axis=0)
    pv = jnp.dot(vt_ones, p, preferred_element_type=F32)
    acc_ref[h] = jnp.exp2(m_old - m_new) * acc_ref[h] + pv
    m_ref[h] = m_new


def _attention_pipeline(last, heads, logits, consume, s_ref):
    assert LOOKAHEAD < heads

    def issue(c, h):
        s_ref[h] = logits(c, h)

    def step(c, is_last):
        for h in range(heads):
            if h + LOOKAHEAD < heads:
                issue(c, h + LOOKAHEAD)
            elif not is_last:
                issue(c + 1, h + LOOKAHEAD - heads)
            consume(c, h, s_ref[h], is_last)

    def body(c, carry):
        step(c, False)
        return carry

    for h in range(LOOKAHEAD):
        issue(0, h)
    lax.fori_loop(0, last, body, 0)
    step(last, True)


def _softmax_init(m_ref, acc_ref):
    m_ref[...] = jnp.full(m_ref.shape, NEG, F32)
    acc_ref[...] = jnp.zeros(acc_ref.shape, F32)


def _softmax_finish(acc_ref, g_ref, o_ref, heads, dv):
    for h in range(heads):
        acc = acc_ref[h]
        o = acc[:dv] * (1.0 / acc[dv:dv + 1])
        o_ref[h * dv:(h + 1) * dv, :] = (o * g_ref[h * dv:(h + 1) * dv, :]).astype(o_ref.dtype)


def _dsa_kernel(iq_ref, iwT_ref, ik_ref, qa_ref, ka_ref, vaT_ref, g_ref, o_ref,
                key_ref, bias_ref, m_ref, acc_ref, s_ref, *, k_top):
    C = CHUNK
    i = pl.program_id(1)
    nk = i + 1
    row = lax.broadcasted_iota(I32, (C, C), 0)
    col = lax.broadcasted_iota(I32, (C, C), 1)
    w = iwT_ref[...]

    def chunk_rows(c):
        return pl.ds(pl.multiple_of(c * C, C), C)

    def score_chunk(c, carry):
        ikc = ik_ref[chunk_rows(c), :]
        acc = jnp.zeros((C, C), F32)
        for h in range(IDX_HEADS):
            lg = _nt(ikc, iq_ref[:, h * LANES:(h + 1) * LANES])
            acc = acc + w[h:h + 1, :] * jnp.maximum(lg, 0.0)
        acc = acc + 0.0
        bits = lax.bitcast_convert_type(acc, I32)
        key = jnp.where(bits < 0, bits ^ jnp.int32(0x7FFFFFFF), bits)
        valid = (c * C + row) <= (i * C + col)
        key_ref[chunk_rows(c), :] = jnp.where(valid, key, jnp.int32(INT_MIN))
        return carry

    lax.fori_loop(0, nk, score_chunk, 0)

    def count(pred):
        def body(c, acc8):
            hit = pred(key_ref[chunk_rows(c), :]).astype(I32)
            return acc8 + jnp.sum(hit.reshape(C // 8, 8, C), axis=0)
        acc8 = lax.fori_loop(0, nk, body, jnp.zeros((8, C), I32))
        return jnp.sum(acc8, axis=0, keepdims=True)

    def bit_body(b, v):
        cand = v + jnp.left_shift(jnp.int32(1), 31 - b)
        return jnp.where(count(lambda k: k >= cand) >= k_top, cand, v)

    thr = lax.fori_loop(0, 32, bit_body, jnp.full((1, C), INT_MIN, I32))

    need = (k_top - count(lambda k: k > thr)).astype(F32)
    tri = (row >= col).astype(BF16)

    def bias_chunk(c, carry):
        key = key_ref[chunk_rows(c), :]
        eq = key == thr
        prefix = jnp.dot(tri, eq.astype(BF16), preferred_element_type=F32) + carry
        sel = (key > thr) | (eq & (prefix <= need))
        valid = (c * C + row) <= (i * C + col)
        bias_ref[chunk_rows(c), :] = jnp.where(sel & valid, 0.0, NEG)
        return carry + jnp.sum(eq.astype(F32), axis=0, keepdims=True)

    lax.fori_loop(0, nk, bias_chunk, jnp.zeros((1, C), F32))

    _softmax_init(m_ref, acc_ref)

    def logits(c, h):
        s = _nt(ka_ref[chunk_rows(c), (h // 2) * LANES:(h // 2 + 1) * LANES],
                qa_ref[:, h * LANES:(h + 1) * LANES])
        return s + bias_ref[chunk_rows(c), :]

    def consume(c, h, s, is_last):
        _softmax_step(s, vaT_ref[c, h * A_HEAD_DIM:(h + 1) * A_HEAD_DIM, :], m_ref, acc_ref, h)

    _attention_pipeline(i, A_HEADS, logits, consume, s_ref)
    _softmax_finish(acc_ref, g_ref, o_ref, A_HEADS, A_HEAD_DIM)


def _dsa_attention(iq, iwT, ik, qa, ka, vaT, gT, batch, seq):
    C = CHUNK
    nc = seq // C
    k_top = min(TOPK_MAX, seq // 4)
    return pl.pallas_call(
        functools.partial(_dsa_kernel, k_top=k_top),
        grid=(batch, nc),
        in_specs=[
            pl.BlockSpec((C, IDX_HEADS * LANES), lambda b, i: (b * nc + i, 0)),
            pl.BlockSpec((None, 16, C), lambda b, i: (b * nc + i, 0, 0)),
            pl.BlockSpec((seq, LANES), lambda b, i: (b, 0)),
            pl.BlockSpec((C, A_HEADS * LANES), lambda b, i: (b * nc + i, 0)),
            pl.BlockSpec((seq, A_WIDTH), lambda b, i: (b, 0)),
            pl.BlockSpec((nc, A_WIDTH, C), lambda b, i: (b, 0, 0)),
            pl.BlockSpec((None, A_WIDTH, C), lambda b, i: (b * nc + i, 0, 0)),
        ],
        out_specs=pl.BlockSpec((None, A_WIDTH, C), lambda b, i: (b * nc + i, 0, 0)),
        out_shape=jax.ShapeDtypeStruct((batch * nc, A_WIDTH, C), BF16),
        scratch_shapes=[
            pltpu.VMEM((seq, C), I32),
            pltpu.VMEM((seq, C), F32),
            pltpu.VMEM((A_HEADS, 1, C), F32),
            pltpu.VMEM((A_HEADS, A_HEAD_DIM + L_ROWS, C), F32),
            pltpu.VMEM((A_HEADS, C, C), F32),
        ],
        compiler_params=pltpu.CompilerParams(dimension_semantics=("parallel", "arbitrary"),
                                             vmem_limit_bytes=VMEM_LIMIT),
        name="dsa_attention",
    )(iq, iwT, ik, qa, ka, vaT, gT)


def _mla_kernel(qb_ref, kb_ref, vbT_ref, g_ref, o_ref, m_ref, acc_ref, s_ref):
    C = CHUNK
    i = pl.program_id(1)
    row = lax.broadcasted_iota(I32, (C, C), 0)
    col = lax.broadcasted_iota(I32, (C, C), 1)
    causal = row <= col
    _softmax_init(m_ref, acc_ref)

    def logits(c, h):
        kc = kb_ref[pl.ds(pl.multiple_of(c * C, C), C), h * LANES:(h + 1) * LANES]
        return _nt(kc, qb_ref[:, h * LANES:(h + 1) * LANES])

    def consume(c, h, s, is_last):
        if is_last:
            s = jnp.where(causal, s, NEG)
        _softmax_step(s, vbT_ref[c, h * B_V_DIM:(h + 1) * B_V_DIM, :], m_ref, acc_ref, h)

    _attention_pipeline(i, B_HEADS, logits, consume, s_ref)
    _softmax_finish(acc_ref, g_ref, o_ref, B_HEADS, B_V_DIM)


def _mla_attention(qb, kb, vbT, gT, batch, seq):
    C = CHUNK
    nc = seq // C
    return pl.pallas_call(
        _mla_kernel,
        grid=(batch, nc),
        in_specs=[
            pl.BlockSpec((C, B_HEADS * LANES), lambda b, i: (b * nc + i, 0)),
            pl.BlockSpec((seq, B_HEADS * LANES), lambda b, i: (b, 0)),
            pl.BlockSpec((nc, B_WIDTH, C), lambda b, i: (b, 0, 0)),
            pl.BlockSpec((None, B_WIDTH, C), lambda b, i: (b * nc + i, 1, 0)),
        ],
        out_specs=pl.BlockSpec((None, B_WIDTH, C), lambda b, i: (b * nc + i, 0, 0)),
        out_shape=jax.ShapeDtypeStruct((batch * nc, B_WIDTH, C), BF16),
        scratch_shapes=[
            pltpu.VMEM((B_HEADS, 1, C), F32),
            pltpu.VMEM((B_HEADS, B_V_DIM + L_ROWS, C), F32),
            pltpu.VMEM((B_HEADS, C, C), F32),
        ],
        compiler_params=pltpu.CompilerParams(dimension_semantics=("parallel", "arbitrary"),
                                             vmem_limit_bytes=VMEM_LIMIT),
        name="mla_attention",
    )(qb, kb, vbT, gT)


def _out_kernel(oa_ref, ob_ref, x_ref, w_ref, lng_ref, lnb_ref, y_ref, *, alpha):
    out = (lax.dot_general(oa_ref[...], w_ref[:A_WIDTH, :], TN_DIMS, preferred_element_type=F32)
           + lax.dot_general(ob_ref[...], w_ref[A_WIDTH:, :], TN_DIMS, preferred_element_type=F32))
    z = alpha * x_ref[...] + out
    mu = jnp.mean(z, axis=-1, keepdims=True)
    zc = z - mu
    var = jnp.mean(zc * zc, axis=-1, keepdims=True)
    y_ref[...] = zc * lax.rsqrt(var + LN_EPS) * lng_ref[...] + lnb_ref[...]


def _output(oaT, obT, x2, w_out, ln_g, ln_b, alpha):
    C = CHUNK
    n = x2.shape[0]
    const = lambda a: pl.BlockSpec(a.shape, lambda t: (0,) * a.ndim)
    return pl.pallas_call(
        functools.partial(_out_kernel, alpha=alpha),
        grid=(n // C,),
        in_specs=[
            pl.BlockSpec((None, A_WIDTH, C), lambda t: (t, 0, 0)),
            pl.BlockSpec((None, B_WIDTH, C), lambda t: (t, 0, 0)),
            pl.BlockSpec((C, D_MODEL), lambda t: (t, 0)),
            const(w_out), const(ln_g), const(ln_b),
        ],
        out_specs=pl.BlockSpec((C, D_MODEL), lambda t: (t, 0)),
        out_shape=jax.ShapeDtypeStruct((n, D_MODEL), F32),
        compiler_params=pltpu.CompilerParams(dimension_semantics=("parallel",),
                                             vmem_limit_bytes=VMEM_LIMIT),
        name="output",
    )(oaT, obT, x2, w_out, ln_g, ln_b)


def _prepare_weights(w_in, q_norm_g, w_uq, kv_norm_g, w_ukv):
    offs = np.concatenate([[0], np.cumsum(IN_SPLITS)])
    (wqa, wka, wva, wga, wiq, wik, wiw, wcq, wckv, wkr, wgb) = [
        w_in[:, int(offs[j]):int(offs[j + 1])] for j in range(len(IN_SPLITS))]
    d = w_in.shape[0]
    zeros = lambda c: jnp.zeros((d, c), w_in.dtype)
    wik4 = jnp.concatenate([wik] * (LANES // IDX_DIM), axis=1)
    wkpe = jnp.concatenate([zeros(B_NOPE_DIM), wkr, zeros(LANES - B_NOPE_DIM - B_ROPE_DIM)], axis=1)
    wiwT = jnp.concatenate([wiw.T, jnp.zeros((16 - IDX_HEADS, d), w_in.dtype)], axis=0)
    wgT = jnp.concatenate([wga, wgb], axis=1).T

    qk = B_NOPE_DIM + B_ROPE_DIM
    wuq = w_uq.reshape(Q_LORA, B_HEADS, qk)
    wuq = jnp.pad(wuq, ((0, 0), (0, 0), (0, LANES - qk))).reshape(Q_LORA, B_HEADS * LANES)
    wukv = w_ukv.reshape(KV_LORA, B_HEADS, B_NOPE_DIM + B_V_DIM)
    wukvk = jnp.pad(wukv[:, :, :B_NOPE_DIM], ((0, 0), (0, 0), (0, LANES - B_NOPE_DIM)))
    wukvk = wukvk.reshape(KV_LORA, B_HEADS * LANES)
    wukvvT = wukv[:, :, B_NOPE_DIM:].reshape(KV_LORA, B_WIDTH).T

    bf = lambda a: a.astype(BF16)
    return [bf(wqa), bf(wka), bf(wiq), bf(wik4), bf(wkpe), bf(wcq), bf(wckv),
            bf(wva.T), bf(wgT), bf(wiwT),
            q_norm_g.reshape(1, Q_LORA).astype(F32), kv_norm_g.reshape(1, KV_LORA).astype(F32),
            bf(wuq), bf(wukvk), bf(wukvvT)]


def kernel(x, positions, w_in, q_norm_g, w_uq, kv_norm_g, w_ukv, w_out, ln_g, ln_b):
    batch, seq, d_model = x.shape
    depth = w_in.shape[0]
    alpha = (2 * depth) ** 0.25
    assert d_model == D_MODEL and seq % ROW_TILE == 0 and ROW_TILE % CHUNK == 0
    n = batch * seq
    pos_col = positions.reshape(n, 1).astype(I32)
    x2 = x.reshape(n, d_model)
    for l in range(depth):
        weights = _prepare_weights(w_in[l], q_norm_g[l], w_uq[l], kv_norm_g[l], w_ukv[l])
        qa, ka, iq, ik, qb, kb, vaT, vbT, gT, iwT = _projection(x2, pos_col, weights)
        oaT = _dsa_attention(iq, iwT, ik, qa, ka, vaT, gT, batch, seq)
        obT = _mla_attention(qb, kb, vbT, gT, batch, seq)
        x2 = _output(oaT, obT, x2, w_out[l].astype(BF16),
                     ln_g[l].reshape(1, d_model), ln_b[l].reshape(1, d_model), alpha)
    return x2.reshape(batch, seq, d_model)
```

```python
import functools

import jax
import jax.numpy as jnp
import numpy as np
from jax import lax
from jax.experimental import pallas as pl
from jax.experimental.pallas import tpu as pltpu

F32 = jnp.float32
BF16 = jnp.bfloat16
I32 = jnp.int32
I16 = jnp.int16

D_MODEL = 1024
A_HEADS = 8
A_HEAD_DIM = 64
A_WIDTH = A_HEADS * A_HEAD_DIM
IDX_HEADS = 8
IDX_DIM = 32
TOPK_MAX = 256
B_HEADS = 8
B_NOPE_DIM = 64
B_ROPE_DIM = 32
B_V_DIM = 64
B_WIDTH = B_HEADS * B_V_DIM
Q_LORA = 256
KV_LORA = 128
ROPE_THETA = 10000.0
LN_EPS = 1e-5
RMS_EPS = 1e-6

IN_SPLITS = (A_WIDTH, A_WIDTH, A_WIDTH, A_WIDTH, IDX_HEADS * IDX_DIM, IDX_DIM, IDX_HEADS,
             Q_LORA, KV_LORA, B_ROPE_DIM, B_WIDTH)

LANES = 128
CHUNK = 256
ROW_TILE = 512
VMEM_LIMIT = 56 * 1024 * 1024
NEG = -1e30
INT_MIN = -2 ** 31
MIN16 = -2 ** 15
PACK = 16
LOG2E = 1.4426950408889634
L_ROWS = 16
LOOKAHEAD = 4

NT_DIMS = (((1,), (1,)), ((), ()))
TN_DIMS = (((0,), (0,)), ((), ()))


def _nt(a, b):
    return lax.dot_general(a, b, NT_DIMS, preferred_element_type=F32)


FREQ_BITS = 12


def _rope_patterns():
    lane = np.arange(LANES)
    rows = []
    for d in (A_HEAD_DIM, IDX_DIM):
        inv = (ROPE_THETA ** (-np.arange(0, d, 2, dtype=np.float64) / d))[lane % (d // 2)]
        rest = inv
        for _ in range(2):
            mant, exp = np.frexp(rest)
            piece = np.ldexp(np.floor(mant * 2 ** FREQ_BITS) / 2 ** FREQ_BITS, exp)
            rows.append(piece)
            rest = rest - piece
        rows.append(rest)
        rows.append(np.where((lane % d) < d // 2, -1.0, 1.0))
    return jnp.asarray(np.stack(rows), F32)


def _rope_tables(pos_ref, pat_ref):
    pos = pos_ref[...].astype(F32)
    tables = []
    for r in (0, 4):
        a0 = pos * pat_ref[r:r + 1, :]
        dl = pos * pat_ref[r + 1:r + 2, :] + pos * pat_ref[r + 2:r + 3, :]
        c0, s0, cd, sd = jnp.cos(a0), jnp.sin(a0), jnp.cos(dl), jnp.sin(dl)
        tables += [c0 * cd - s0 * sd, (s0 * cd + c0 * sd) * pat_ref[r + 3:r + 4, :]]
    return tables


def _rope_group(xg, cos, sin_signed, low, half):
    swapped = jnp.where(low, pltpu.roll(xg, LANES - half, 1), pltpu.roll(xg, half, 1))
    return xg * cos + swapped * sin_signed


def _proj_kernel(x_ref, pos_ref, pat_ref,
                 wq_ref, wk_ref, widx_ref, wkpe_ref, wcq_ref, wckv_ref,
                 wvT_ref, wgT_ref, gq_ref, gkv_ref, wuq_ref, wukvk_ref, wukvvT_ref,
                 qa_ref, ka_ref, iq_ref, ik_ref, qb_ref, kb_ref,
                 vaT_ref, vbT_ref, gT_ref, iwT_ref):
    tm = x_ref.shape[0]
    xb = x_ref[...].astype(BF16)
    c64, s64, c32, s32 = _rope_tables(pos_ref, pat_ref)
    lane = lax.broadcasted_iota(I32, (tm, LANES), 1)
    low64 = (lane % A_HEAD_DIM) < A_HEAD_DIM // 2
    low32 = (lane % IDX_DIM) < IDX_DIM // 2

    def dot(w_ref):
        return jnp.dot(xb, w_ref[...], preferred_element_type=F32)

    def rope_all(h, cos, sin, low, half, out_ref, scale=None, head_dim=None):
        for g in range(h.shape[1] // LANES):
            r = _rope_group(h[:, g * LANES:(g + 1) * LANES], cos, sin, low, half)
            if scale is not None:
                r = r * scale
            if head_dim is None:
                out_ref[:, g * LANES:(g + 1) * LANES] = r.astype(out_ref.dtype)
                continue
            per_group = LANES // head_dim
            for j in range(per_group):
                o = (g * per_group + j) * LANES
                out_ref[:, o:o + LANES] = jnp.where(lane // head_dim == j, r, 0.0).astype(out_ref.dtype)

    rope_all(dot(wq_ref), c64, s64, low64, A_HEAD_DIM // 2, qa_ref, A_HEAD_DIM ** -0.5 * LOG2E,
             head_dim=A_HEAD_DIM)
    rope_all(dot(wk_ref), c64, s64, low64, A_HEAD_DIM // 2, ka_ref)

    hidx = jnp.dot(x_ref[...], widx_ref[...], precision=lax.Precision.HIGHEST,
                   preferred_element_type=F32)
    slot = lane // IDX_DIM

    def split3(v):
        hi = v.astype(BF16).astype(F32)
        mid = (v - hi).astype(BF16).astype(F32)
        lo = (v - hi - mid).astype(BF16).astype(F32)
        return hi, mid, lo

    def idx_rope(g):
        return _rope_group(hidx[:, g * LANES:(g + 1) * LANES], c32, s32, low32, IDX_DIM // 2)

    k_hi, k_mid, k_lo = split3(idx_rope(2))
    ik_ref[:, :LANES] = jnp.where(slot == 1, k_mid, jnp.where(slot == 3, k_lo, k_hi)).astype(BF16)
    ik_ref[:, LANES:] = jnp.where(slot == 0, k_hi, jnp.where(slot == 1, k_mid, 0.0)).astype(BF16)
    slots = LANES // IDX_DIM
    for g in range(IDX_HEADS // slots):
        rolled = [[p if d == 0 else pltpu.roll(p, d * IDX_DIM, 1) for d in range(slots)]
                  for p in split3(idx_rope(g))]
        for s_ in range(slots):
            at = lambda piece, t: rolled[piece][(t - s_) % slots]
            b1 = jnp.where(slot == 0, at(0, 0), jnp.where(slot == 1, at(0, 1),
                           jnp.where(slot == 2, at(1, 2), at(0, 3))))
            b2 = jnp.where(slot == 0, at(2, 0), jnp.where(slot == 1, at(1, 1), 0.0))
            o = (g * slots + s_) * 2 * LANES
            iq_ref[:, o:o + LANES] = b1.astype(BF16)
            iq_ref[:, o + LANES:o + 2 * LANES] = b2.astype(BF16)

    def rmsnorm(c, g_ref):
        ms = jnp.mean(c * c, axis=-1, keepdims=True)
        return (c * lax.rsqrt(ms + RMS_EPS) * g_ref[...]).astype(BF16)

    cqn = rmsnorm(dot(wcq_ref), gq_ref)
    qb = jnp.dot(cqn, wuq_ref[...], preferred_element_type=F32)
    pe_lane = (lane >= B_NOPE_DIM) & (lane < B_NOPE_DIM + B_ROPE_DIM)
    b_scale = (B_NOPE_DIM + B_ROPE_DIM) ** -0.5 * LOG2E
    for h in range(B_HEADS):
        g = qb[:, h * LANES:(h + 1) * LANES]
        r = jnp.where(pe_lane, _rope_group(g, c32, s32, low32, B_ROPE_DIM // 2), g)
        qb_ref[:, h * LANES:(h + 1) * LANES] = (r * b_scale).astype(BF16)

    ckvn = rmsnorm(dot(wckv_ref), gkv_ref)
    kpe = _rope_group(dot(wkpe_ref), c32, s32, low32, B_ROPE_DIM // 2)
    kn = jnp.dot(ckvn, wukvk_ref[...], preferred_element_type=F32)
    for h in range(B_HEADS):
        kb_ref[:, h * LANES:(h + 1) * LANES] = (kn[:, h * LANES:(h + 1) * LANES] + kpe).astype(BF16)

    def store_t(val, out_ref):
        for j in range(tm // CHUNK):
            out_ref[j] = val[:, j * CHUNK:(j + 1) * CHUNK].astype(out_ref.dtype)

    store_t(_nt(wukvvT_ref[...], ckvn), vbT_ref)
    store_t(_nt(wvT_ref[...], xb), vaT_ref)
    gt = _nt(wgT_ref[...], xb)
    store_t(gt * jax.nn.sigmoid(gt), gT_ref)
    iw_t = jnp.transpose(hidx[:, 3 * LANES:4 * LANES])
    store_t(iw_t[:16] * (IDX_DIM ** -0.5 * IDX_HEADS ** -0.5), iwT_ref)


def _projection(x2, pos_col, weights):
    n = x2.shape[0]
    tm = ROW_TILE
    nchunks = n // CHUNK
    cpt = tm // CHUNK

    def rows(width):
        return pl.BlockSpec((tm, width), lambda t: (t, 0))

    def full(a):
        return pl.BlockSpec(a.shape, lambda t: (0,) * a.ndim)

    def tspec(feat):
        return pl.BlockSpec((cpt, feat, CHUNK), lambda t: (t, 0, 0))

    out_shape = [
        jax.ShapeDtypeStruct((n, A_HEADS * LANES), BF16),
        jax.ShapeDtypeStruct((n, A_WIDTH), BF16),
        jax.ShapeDtypeStruct((n, IDX_HEADS * 2 * LANES), BF16),
        jax.ShapeDtypeStruct((n, 2 * LANES), BF16),
        jax.ShapeDtypeStruct((n, B_HEADS * LANES), BF16),
        jax.ShapeDtypeStruct((n, B_HEADS * LANES), BF16),
        jax.ShapeDtypeStruct((nchunks, A_WIDTH, CHUNK), BF16),
        jax.ShapeDtypeStruct((nchunks, B_WIDTH, CHUNK), BF16),
        jax.ShapeDtypeStruct((nchunks, A_WIDTH + B_WIDTH, CHUNK), F32),
        jax.ShapeDtypeStruct((nchunks, 16, CHUNK), F32),
    ]
    out_specs = [rows(A_HEADS * LANES), rows(A_WIDTH), rows(IDX_HEADS * 2 * LANES), rows(2 * LANES),
                 rows(B_HEADS * LANES), rows(B_HEADS * LANES),
                 tspec(A_WIDTH), tspec(B_WIDTH), tspec(A_WIDTH + B_WIDTH), tspec(16)]
    patterns = _rope_patterns()
    in_specs = [rows(D_MODEL), rows(1), full(patterns)] + [full(w) for w in weights]
    return pl.pallas_call(
        _proj_kernel,
        grid=(n // tm,),
        in_specs=in_specs,
        out_specs=out_specs,
        out_shape=out_shape,
        compiler_params=pltpu.CompilerParams(dimension_semantics=("parallel",),
                                             vmem_limit_bytes=VMEM_LIMIT),
        name="projection",
    )(x2, pos_col, patterns, *weights)


def _softmax_step(s, vt, m_ref, acc_ref, h):
    m_old = m_ref[h]
    m_new = jnp.maximum(m_old, jnp.max(s, axis=0, keepdims=True))
    p = jnp.exp2(s - m_new).astype(BF16)
    vt_ones = jnp.concatenate([vt, jnp.ones((L_ROWS, vt.shape[1]), BF16)], axis=0)
    pv = jnp.dot(vt_ones, p, preferred_element_type=F32)
    acc_ref[h] = jnp.exp2(m_old - m_new) * acc_ref[h] + pv
    m_ref[h] = m_new


def _attention_pipeline(last, heads, logits, consume, s_ref):
    assert LOOKAHEAD < heads

    def issue(c, h):
        s_ref[h] = logits(c, h)

    def step(c, is_last):
        for h in range(heads):
            if h + LOOKAHEAD < heads:
                issue(c, h + LOOKAHEAD)
            elif not is_last:
                issue(c + 1, h + LOOKAHEAD - heads)
            consume(c, h, s_ref[h], is_last)

    def body(c, carry):
        step(c, False)
        return carry

    for h in range(LOOKAHEAD):
        issue(0, h)
    lax.fori_loop(0, last, body, 0)
    step(last, True)


def _softmax_init(m_ref, acc_ref):
    m_ref[...] = jnp.full(m_ref.shape, NEG, F32)
    acc_ref[...] = jnp.zeros(acc_ref.shape, F32)


def _softmax_finish(acc_ref, g_ref, o_ref, heads, dv):
    for h in range(heads):
        acc = acc_ref[h]
        o = acc[:dv] * (1.0 / acc[dv:dv + 1])
        o_ref[h * dv:(h + 1) * dv, :] = (o * g_ref[h * dv:(h + 1) * dv, :]).astype(o_ref.dtype)


def _dsa_kernel(iq_ref, iwT_ref, ik_ref, qa_ref, ka_ref, vaT_ref, g_ref, o_ref,
                hi_ref, lo_ref, bk_ref, bias_ref, m_ref, acc_ref, s_ref, *, k_top):
    C = CHUNK
    i = pl.program_id(1)
    nk = i + 1
    row = lax.broadcasted_iota(I32, (C, C), 0)
    col = lax.broadcasted_iota(I32, (C, C), 1)
    w = iwT_ref[...]

    def chunk_rows(c):
        return pl.ds(pl.multiple_of(c * C, C), C)

    def score_chunk(c, carry):
        ikc = ik_ref[chunk_rows(c), :]
        acc = jnp.zeros((C, C), F32)
        for h in range(IDX_HEADS):
            lg = _nt(ikc, iq_ref[:, h * 2 * LANES:(h + 1) * 2 * LANES])
            acc = acc + w[h:h + 1, :] * jnp.maximum(lg, 0.0)
        acc = acc + 0.0
        bits = lax.bitcast_convert_type(acc, I32)
        key = jnp.where(bits < 0, bits ^ jnp.int32(0x7FFFFFFF), bits)
        valid = (c * C + row) <= (i * C + col)
        key = jnp.where(valid, key, jnp.int32(INT_MIN))
        hi_ref[chunk_rows(c), :] = (key >> 16).astype(I16)
        lo_ref[chunk_rows(c), :] = (key ^ jnp.int32(0x8000)).astype(I16)
        return carry

    lax.fori_loop(0, nk, score_chunk, 0)

    @pl.when(nk % 2 == 1)
    def _():
        hi_ref[chunk_rows(nk), :] = jnp.full((C, C), MIN16, I16)
        lo_ref[chunk_rows(nk), :] = jnp.full((C, C), MIN16, I16)

    n_pairs = (nk + 1) // 2

    def pair_rows(d):
        return pl.ds(pl.multiple_of(d * 2 * C, 2 * C), 2 * C)

    def pack16(v):
        return jnp.broadcast_to(v, (PACK, C)).astype(I16)

    def tiles(x):
        return [x[j * PACK:(j + 1) * PACK] for j in range(x.shape[0] // PACK)]

    def tree_sum(parts):
        while len(parts) > 1:
            parts = [parts[j] + parts[j + 1] for j in range(0, len(parts), 2)]
        return parts[0]

    def count16(ref, pred):
        def body(d, acc):
            hits = [jnp.where(pred(t), jnp.int16(1), jnp.int16(0)) for t in tiles(ref[pair_rows(d), :])]
            return acc + tree_sum(hits)
        acc = lax.fori_loop(0, n_pairs, body, jnp.zeros((PACK, C), I16))
        return jnp.sum(acc.astype(I32), axis=0, keepdims=True)

    def kth_largest(ref, k):
        def bit_body(b, v):
            cand = v + jnp.left_shift(jnp.int32(1), 15 - b)
            c16 = pack16(cand)
            return jnp.where(count16(ref, lambda t: t >= c16) >= k, cand, v)
        return lax.fori_loop(0, 16, bit_body, jnp.full((1, C), MIN16, I32))

    thr_hi = kth_largest(hi_ref, k_top)
    hi16 = pack16(thr_hi)

    def bucket_pair(d, acc):
        his, los = tiles(hi_ref[pair_rows(d), :]), tiles(lo_ref[pair_rows(d), :])
        for j, (th, tl) in enumerate(zip(his, los)):
            bk_ref[pl.ds(pl.multiple_of(d * 2 * C, 2 * C) + j * PACK, PACK), :] = jnp.where(
                th == hi16, tl, jnp.int16(MIN16))
        return acc + tree_sum([jnp.where(th > hi16, jnp.int16(1), jnp.int16(0)) for th in his])

    above = lax.fori_loop(0, n_pairs, bucket_pair, jnp.zeros((PACK, C), I16))
    rank = k_top - jnp.sum(above.astype(I32), axis=0, keepdims=True)
    thr_lo = kth_largest(bk_ref, rank)
    lo16 = pack16(thr_lo)

    need = (rank - count16(bk_ref, lambda t: t > lo16)).astype(F32)
    tri = (row >= col).astype(BF16)

    def bias_chunk(c, carry):
        hi = hi_ref[chunk_rows(c), :].astype(I32)
        lo = lo_ref[chunk_rows(c), :].astype(I32)
        same_hi = hi == thr_hi
        eq = same_hi & (lo == thr_lo)
        gt = (hi > thr_hi) | (same_hi & (lo > thr_lo))
        prefix = jnp.dot(tri, eq.astype(BF16), preferred_element_type=F32) + carry
        sel = gt | (eq & (prefix <= need))
        valid = (c * C + row) <= (i * C + col)
        bias_ref[chunk_rows(c), :] = jnp.where(sel & valid, 0.0, NEG)
        return carry + jnp.sum(eq.astype(F32), axis=0, keepdims=True)

    lax.fori_loop(0, nk, bias_chunk, jnp.zeros((1, C), F32))

    _softmax_init(m_ref, acc_ref)

    def logits(c, h):
        s = _nt(ka_ref[chunk_rows(c), (h // 2) * LANES:(h // 2 + 1) * LANES],
                qa_ref[:, h * LANES:(h + 1) * LANES])
        return s + bias_ref[chunk_rows(c), :]

    def consume(c, h, s, is_last):
        _softmax_step(s, vaT_ref[c, h * A_HEAD_DIM:(h + 1) * A_HEAD_DIM, :], m_ref, acc_ref, h)

    _attention_pipeline(i, A_HEADS, logits, consume, s_ref)
    _softmax_finish(acc_ref, g_ref, o_ref, A_HEADS, A_HEAD_DIM)


def _dsa_attention(iq, iwT, ik, qa, ka, vaT, gT, batch, seq):
    C = CHUNK
    nc = seq // C
    k_top = min(TOPK_MAX, seq // 4)
    sel_rows = (nc + nc % 2) * C
    return pl.pallas_call(
        functools.partial(_dsa_kernel, k_top=k_top),
        grid=(batch, nc),
        in_specs=[
            pl.BlockSpec((C, IDX_HEADS * 2 * LANES), lambda b, i: (b * nc + i, 0)),
            pl.BlockSpec((None, 16, C), lambda b, i: (b * nc + i, 0, 0)),
            pl.BlockSpec((seq, 2 * LANES), lambda b, i: (b, 0)),
            pl.BlockSpec((C, A_HEADS * LANES), lambda b, i: (b * nc + i, 0)),
            pl.BlockSpec((seq, A_WIDTH), lambda b, i: (b, 0)),
            pl.BlockSpec((nc, A_WIDTH, C), lambda b, i: (b, 0, 0)),
            pl.BlockSpec((None, A_WIDTH, C), lambda b, i: (b * nc + i, 0, 0)),
        ],
        out_specs=pl.BlockSpec((None, A_WIDTH, C), lambda b, i: (b * nc + i, 0, 0)),
        out_shape=jax.ShapeDtypeStruct((batch * nc, A_WIDTH, C), BF16),
        scratch_shapes=[
            pltpu.VMEM((sel_rows, C), I16),
            pltpu.VMEM((sel_rows, C), I16),
            pltpu.VMEM((sel_rows, C), I16),
            pltpu.VMEM((seq, C), F32),
            pltpu.VMEM((A_HEADS, 1, C), F32),
            pltpu.VMEM((A_HEADS, A_HEAD_DIM + L_ROWS, C), F32),
            pltpu.VMEM((A_HEADS, C, C), F32),
        ],
        compiler_params=pltpu.CompilerParams(dimension_semantics=("parallel", "arbitrary"),
                                             vmem_limit_bytes=VMEM_LIMIT),
        name="dsa_attention",
    )(iq, iwT, ik, qa, ka, vaT, gT)


def _mla_kernel(qb_ref, kb_ref, vbT_ref, g_ref, o_ref, m_ref, acc_ref, s_ref):
    C = CHUNK
    i = pl.program_id(1)
    row = lax.broadcasted_iota(I32, (C, C), 0)
    col = lax.broadcasted_iota(I32, (C, C), 1)
    causal = row <= col
    _softmax_init(m_ref, acc_ref)

    def logits(c, h):
        kc = kb_ref[pl.ds(pl.multiple_of(c * C, C), C), h * LANES:(h + 1) * LANES]
        return _nt(kc, qb_ref[:, h * LANES:(h + 1) * LANES])

    def consume(c, h, s, is_last):
        if is_last:
            s = jnp.where(causal, s, NEG)
        _softmax_step(s, vbT_ref[c, h * B_V_DIM:(h + 1) * B_V_DIM, :], m_ref, acc_ref, h)

    _attention_pipeline(i, B_HEADS, logits, consume, s_ref)
    _softmax_finish(acc_ref, g_ref, o_ref, B_HEADS, B_V_DIM)


def _mla_attention(qb, kb, vbT, gT, batch, seq):
    C = CHUNK
    nc = seq // C
    return pl.pallas_call(
        _mla_kernel,
        grid=(batch, nc),
        in_specs=[
            pl.BlockSpec((C, B_HEADS * LANES), lambda b, i: (b * nc + i, 0)),
            pl.BlockSpec((seq, B_HEADS * LANES), lambda b, i: (b, 0)),
            pl.BlockSpec((nc, B_WIDTH, C), lambda b, i: (b, 0, 0)),
            pl.BlockSpec((None, B_WIDTH, C), lambda b, i: (b * nc + i, 1, 0)),
        ],
        out_specs=pl.BlockSpec((None, B_WIDTH, C), lambda b, i: (b * nc + i, 0, 0)),
        out_shape=jax.ShapeDtypeStruct((batch * nc, B_WIDTH, C), BF16),
        scratch_shapes=[
            pltpu.VMEM((B_HEADS, 1, C), F32),
            pltpu.VMEM((B_HEADS, B_V_DIM + L_ROWS, C), F32),
            pltpu.VMEM((B_HEADS, C, C), F32),
        ],
        compiler_params=pltpu.CompilerParams(dimension_semantics=("parallel", "arbitrary"),
                                             vmem_limit_bytes=VMEM_LIMIT),
        name="mla_attention",
    )(qb, kb, vbT, gT)


def _out_kernel(oa_ref, ob_ref, x_ref, w_ref, lng_ref, lnb_ref, y_ref, *, alpha):
    out = (lax.dot_general(oa_ref[...], w_ref[:A_WIDTH, :], TN_DIMS, preferred_element_type=F32)
           + lax.dot_general(ob_ref[...], w_ref[A_WIDTH:, :], TN_DIMS, preferred_element_type=F32))
    z = alpha * x_ref[...] + out
    mu = jnp.mean(z, axis=-1, keepdims=True)
    zc = z - mu
    var = jnp.mean(zc * zc, axis=-1, keepdims=True)
    y_ref[...] = zc * lax.rsqrt(var + LN_EPS) * lng_ref[...] + lnb_ref[...]


def _output(oaT, obT, x2, w_out, ln_g, ln_b, alpha):
    C = CHUNK
    n = x2.shape[0]
    const = lambda a: pl.BlockSpec(a.shape, lambda t: (0,) * a.ndim)
    return pl.pallas_call(
        functools.partial(_out_kernel, alpha=alpha),
        grid=(n // C,),
        in_specs=[
            pl.BlockSpec((None, A_WIDTH, C), lambda t: (t, 0, 0)),
            pl.BlockSpec((None, B_WIDTH, C), lambda t: (t, 0, 0)),
            pl.BlockSpec((C, D_MODEL), lambda t: (t, 0)),
            const(w_out), const(ln_g), const(ln_b),
        ],
        out_specs=pl.BlockSpec((C, D_MODEL), lambda t: (t, 0)),
        out_shape=jax.ShapeDtypeStruct((n, D_MODEL), F32),
        compiler_params=pltpu.CompilerParams(dimension_semantics=("parallel",),
                                             vmem_limit_bytes=VMEM_LIMIT),
        name="output",
    )(oaT, obT, x2, w_out, ln_g, ln_b)


def _prepare_weights(w_in, q_norm_g, w_uq, kv_norm_g, w_ukv):
    offs = np.concatenate([[0], np.cumsum(IN_SPLITS)])
    (wqa, wka, wva, wga, wiq, wik, wiw, wcq, wckv, wkr, wgb) = [
        w_in[:, int(offs[j]):int(offs[j + 1])] for j in range(len(IN_SPLITS))]
    d = w_in.shape[0]
    zeros = lambda c: jnp.zeros((d, c), w_in.dtype)
    widx = jnp.concatenate([wiq] + [wik] * (LANES // IDX_DIM) + [wiw, zeros(LANES - IDX_HEADS)], axis=1)
    wkpe = jnp.concatenate([zeros(B_NOPE_DIM), wkr, zeros(LANES - B_NOPE_DIM - B_ROPE_DIM)], axis=1)
    wgT = jnp.concatenate([wga, wgb], axis=1).T

    qk = B_NOPE_DIM + B_ROPE_DIM
    wuq = w_uq.reshape(Q_LORA, B_HEADS, qk)
    wuq = jnp.pad(wuq, ((0, 0), (0, 0), (0, LANES - qk))).reshape(Q_LORA, B_HEADS * LANES)
    wukv = w_ukv.reshape(KV_LORA, B_HEADS, B_NOPE_DIM + B_V_DIM)
    wukvk = jnp.pad(wukv[:, :, :B_NOPE_DIM], ((0, 0), (0, 0), (0, LANES - B_NOPE_DIM)))
    wukvk = wukvk.reshape(KV_LORA, B_HEADS * LANES)
    wukvvT = wukv[:, :, B_NOPE_DIM:].reshape(KV_LORA, B_WIDTH).T

    bf = lambda a: a.astype(BF16)
    return [bf(wqa), bf(wka), widx.astype(F32), bf(wkpe), bf(wcq), bf(wckv),
            bf(wva.T), bf(wgT),
            q_norm_g.reshape(1, Q_LORA).astype(F32), kv_norm_g.reshape(1, KV_LORA).astype(F32),
            bf(wuq), bf(wukvk), bf(wukvvT)]


def kernel(x, positions, w_in, q_norm_g, w_uq, kv_norm_g, w_ukv, w_out, ln_g, ln_b):
    batch, seq, d_model = x.shape
    depth = w_in.shape[0]
    alpha = (2 * depth) ** 0.25
    assert d_model == D_MODEL and seq % ROW_TILE == 0 and ROW_TILE % CHUNK == 0
    n = batch * seq
    pos_col = positions.reshape(n, 1).astype(I32)
    x2 = x.reshape(n, d_model)
    for l in range(depth):
        weights = _prepare_weights(w_in[l], q_norm_g[l], w_uq[l], kv_norm_g[l], w_ukv[l])
        qa, ka, iq, ik, qb, kb, vaT, vbT, gT, iwT = _projection(x2, pos_col, weights)
        oaT = _dsa_attention(iq, iwT, ik, qa, ka, vaT, gT, batch, seq)
        obT = _mla_attention(qb, kb, vbT, gT, batch, seq)
        x2 = _output(oaT, obT, x2, w_out[l].astype(BF16),
                     ln_g[l].reshape(1, d_model), ln_b[l].reshape(1, d_model), alpha)
    return x2.reshape(batch, seq, d_model)
```

```python
import functools

import jax
import jax.numpy as jnp
import numpy as np
from jax import lax
from jax.experimental import pallas as pl
from jax.experimental.pallas import tpu as pltpu

F32 = jnp.float32
BF16 = jnp.bfloat16
I32 = jnp.int32
I16 = jnp.int16

D_MODEL = 1024
A_HEADS = 8
A_HEAD_DIM = 64
A_WIDTH = A_HEADS * A_HEAD_DIM
IDX_HEADS = 8
IDX_DIM = 32
TOPK_MAX = 256
B_HEADS = 8
B_NOPE_DIM = 64
B_ROPE_DIM = 32
B_V_DIM = 64
B_WIDTH = B_HEADS * B_V_DIM
Q_LORA = 256
KV_LORA = 128
ROPE_THETA = 10000.0
LN_EPS = 1e-5
RMS_EPS = 1e-6

IN_SPLITS = (A_WIDTH, A_WIDTH, A_WIDTH, A_WIDTH, IDX_HEADS * IDX_DIM, IDX_DIM, IDX_HEADS,
             Q_LORA, KV_LORA, B_ROPE_DIM, B_WIDTH)

LANES = 128
CHUNK = 256
ROW_TILE = 512
VMEM_LIMIT = 56 * 1024 * 1024
NEG = -1e30
INT_MIN = -2 ** 31
MIN16 = -2 ** 15
PACK = 16
LOG2E = 1.4426950408889634
L_ROWS = 16
LOOKAHEAD = 4

NT_DIMS = (((1,), (1,)), ((), ()))
TN_DIMS = (((0,), (0,)), ((), ()))


def _nt(a, b):
    return lax.dot_general(a, b, NT_DIMS, preferred_element_type=F32)


FREQ_BITS = 12


ROPE_HALF = IDX_DIM // 2


def _head64_order():
    i = np.arange(ROPE_HALF)
    half = A_HEAD_DIM // 2
    return np.concatenate([2 * i, half + 2 * i, 2 * i + 1, half + 2 * i + 1])


def _rope_patterns():
    lane = np.arange(LANES)
    freq_index = 2 * (lane % ROPE_HALF) + (lane // IDX_DIM) % 2
    rest = ROPE_THETA ** (-2.0 * freq_index.astype(np.float64) / A_HEAD_DIM)
    rows = []
    for _ in range(2):
        mant, exp = np.frexp(rest)
        piece = np.ldexp(np.floor(mant * 2 ** FREQ_BITS) / 2 ** FREQ_BITS, exp)
        rows.append(piece)
        rest = rest - piece
    rows.append(rest)
    rows.append(np.where(lane % IDX_DIM < ROPE_HALF, -1.0, 1.0))
    return jnp.asarray(np.stack(rows), F32)


def _rope_tables(pos_ref, pat_ref, lane):
    pos = pos_ref[...].astype(F32)
    a0 = pos * pat_ref[0:1, :]
    dl = pos * pat_ref[1:2, :] + pos * pat_ref[2:3, :]
    c0, s0, cd, sd = jnp.cos(a0), jnp.sin(a0), jnp.cos(dl), jnp.sin(dl)
    c64 = c0 * cd - s0 * sd
    s64 = (s0 * cd + c0 * sd) * pat_ref[3:4, :]
    odd_block = (lane // IDX_DIM) % 2 == 1
    c32 = jnp.where(odd_block, pltpu.roll(c64, IDX_DIM, 1), c64)
    s32 = jnp.where(odd_block, pltpu.roll(s64, IDX_DIM, 1), s64)
    return c64, s64, c32, s32


def _rope_group(xg, cos, sin_signed, low):
    swapped = jnp.where(low, pltpu.roll(xg, LANES - ROPE_HALF, 1), pltpu.roll(xg, ROPE_HALF, 1))
    return xg * cos + swapped * sin_signed


def _proj_kernel(x_ref, pos_ref, pat_ref,
                 wq_ref, wk_ref, widx_ref, wkpe_ref, wcq_ref, wckv_ref,
                 wvT_ref, wgT_ref, gq_ref, gkv_ref, wuq_ref, wukvk_ref, wukvvT_ref,
                 qa_ref, ka_ref, iq_ref, ik_ref, qb_ref, kb_ref,
                 vaT_ref, vbT_ref, gT_ref, iwT_ref):
    tm = x_ref.shape[0]
    xb = x_ref[...].astype(BF16)
    lane = lax.broadcasted_iota(I32, (tm, LANES), 1)
    c64, s64, c32, s32 = _rope_tables(pos_ref, pat_ref, lane)
    low = (lane % IDX_DIM) < ROPE_HALF

    def dot(w_ref):
        return jnp.dot(xb, w_ref[...], preferred_element_type=F32)

    def rope_all(h, cos, sin, out_ref, scale=None, head_dim=None):
        for g in range(h.shape[1] // LANES):
            r = _rope_group(h[:, g * LANES:(g + 1) * LANES], cos, sin, low)
            if scale is not None:
                r = r * scale
            if head_dim is None:
                out_ref[:, g * LANES:(g + 1) * LANES] = r.astype(out_ref.dtype)
                continue
            per_group = LANES // head_dim
            for j in range(per_group):
                o = (g * per_group + j) * LANES
                out_ref[:, o:o + LANES] = jnp.where(lane // head_dim == j, r, 0.0).astype(out_ref.dtype)

    rope_all(dot(wq_ref), c64, s64, qa_ref, A_HEAD_DIM ** -0.5 * LOG2E,
             head_dim=A_HEAD_DIM)
    rope_all(dot(wk_ref), c64, s64, ka_ref)

    hidx = jnp.dot(x_ref[...], widx_ref[...], precision=lax.Precision.HIGHEST,
                   preferred_element_type=F32)
    slot = lane // IDX_DIM

    def split3(v):
        hi = v.astype(BF16).astype(F32)
        mid = (v - hi).astype(BF16).astype(F32)
        lo = (v - hi - mid).astype(BF16).astype(F32)
        return hi, mid, lo

    def idx_rope(g):
        return _rope_group(hidx[:, g * LANES:(g + 1) * LANES], c32, s32, low)

    k_hi, k_mid, k_lo = split3(idx_rope(2))
    ik_ref[:, :LANES] = jnp.where(slot == 1, k_mid, jnp.where(slot == 3, k_lo, k_hi)).astype(BF16)
    ik_ref[:, LANES:] = jnp.where(slot == 0, k_hi, jnp.where(slot == 1, k_mid, 0.0)).astype(BF16)
    slots = LANES // IDX_DIM
    for g in range(IDX_HEADS // slots):
        rolled = [[p if d == 0 else pltpu.roll(p, d * IDX_DIM, 1) for d in range(slots)]
                  for p in split3(idx_rope(g))]
        for s_ in range(slots):
            at = lambda piece, t: rolled[piece][(t - s_) % slots]
            b1 = jnp.where(slot == 0, at(0, 0), jnp.where(slot == 1, at(0, 1),
                           jnp.where(slot == 2, at(1, 2), at(0, 3))))
            b2 = jnp.where(slot == 0, at(2, 0), jnp.where(slot == 1, at(1, 1), 0.0))
            o = (g * slots + s_) * 2 * LANES
            iq_ref[:, o:o + LANES] = b1.astype(BF16)
            iq_ref[:, o + LANES:o + 2 * LANES] = b2.astype(BF16)

    def rmsnorm(c, g_ref):
        ms = jnp.mean(c * c, axis=-1, keepdims=True)
        return (c * lax.rsqrt(ms + RMS_EPS) * g_ref[...]).astype(BF16)

    cqn = rmsnorm(dot(wcq_ref), gq_ref)
    qb = jnp.dot(cqn, wuq_ref[...], preferred_element_type=F32)
    pe_lane = (lane >= B_NOPE_DIM) & (lane < B_NOPE_DIM + B_ROPE_DIM)
    b_scale = (B_NOPE_DIM + B_ROPE_DIM) ** -0.5 * LOG2E
    for h in range(B_HEADS):
        g = qb[:, h * LANES:(h + 1) * LANES]
        r = jnp.where(pe_lane, _rope_group(g, c32, s32, low), g)
        qb_ref[:, h * LANES:(h + 1) * LANES] = (r * b_scale).astype(BF16)

    ckvn = rmsnorm(dot(wckv_ref), gkv_ref)
    kpe = _rope_group(dot(wkpe_ref), c32, s32, low)
    kn = jnp.dot(ckvn, wukvk_ref[...], preferred_element_type=F32)
    for h in range(B_HEADS):
        kb_ref[:, h * LANES:(h + 1) * LANES] = (kn[:, h * LANES:(h + 1) * LANES] + kpe).astype(BF16)

    def store_t(val, out_ref):
        for j in range(tm // CHUNK):
            out_ref[j] = val[:, j * CHUNK:(j + 1) * CHUNK].astype(out_ref.dtype)

    store_t(_nt(wukvvT_ref[...], ckvn), vbT_ref)
    store_t(_nt(wvT_ref[...], xb), vaT_ref)
    gt = _nt(wgT_ref[...], xb)
    store_t(gt * jax.nn.sigmoid(gt), gT_ref)
    iw_t = jnp.transpose(hidx[:, 3 * LANES:4 * LANES])
    store_t(iw_t[:16] * (IDX_DIM ** -0.5 * IDX_HEADS ** -0.5), iwT_ref)


def _projection(x2, pos_col, weights):
    n = x2.shape[0]
    tm = ROW_TILE
    nchunks = n // CHUNK
    cpt = tm // CHUNK

    def rows(width):
        return pl.BlockSpec((tm, width), lambda t: (t, 0))

    def full(a):
        return pl.BlockSpec(a.shape, lambda t: (0,) * a.ndim)

    def tspec(feat):
        return pl.BlockSpec((cpt, feat, CHUNK), lambda t: (t, 0, 0))

    out_shape = [
        jax.ShapeDtypeStruct((n, A_HEADS * LANES), BF16),
        jax.ShapeDtypeStruct((n, A_WIDTH), BF16),
        jax.ShapeDtypeStruct((n, IDX_HEADS * 2 * LANES), BF16),
        jax.ShapeDtypeStruct((n, 2 * LANES), BF16),
        jax.ShapeDtypeStruct((n, B_HEADS * LANES), BF16),
        jax.ShapeDtypeStruct((n, B_HEADS * LANES), BF16),
        jax.ShapeDtypeStruct((nchunks, A_WIDTH, CHUNK), BF16),
        jax.ShapeDtypeStruct((nchunks, B_WIDTH, CHUNK), BF16),
        jax.ShapeDtypeStruct((nchunks, A_WIDTH + B_WIDTH, CHUNK), F32),
        jax.ShapeDtypeStruct((nchunks, 16, CHUNK), F32),
    ]
    out_specs = [rows(A_HEADS * LANES), rows(A_WIDTH), rows(IDX_HEADS * 2 * LANES), rows(2 * LANES),
                 rows(B_HEADS * LANES), rows(B_HEADS * LANES),
                 tspec(A_WIDTH), tspec(B_WIDTH), tspec(A_WIDTH + B_WIDTH), tspec(16)]
    patterns = _rope_patterns()
    in_specs = [rows(D_MODEL), rows(1), full(patterns)] + [full(w) for w in weights]
    return pl.pallas_call(
        _proj_kernel,
        grid=(n // tm,),
        in_specs=in_specs,
        out_specs=out_specs,
        out_shape=out_shape,
        compiler_params=pltpu.CompilerParams(dimension_semantics=("parallel",),
                                             vmem_limit_bytes=VMEM_LIMIT),
        name="projection",
    )(x2, pos_col, patterns, *weights)


def _softmax_step(s, vt, m_ref, acc_ref, h):
    m_old = m_ref[h]
    m_new = jnp.maximum(m_old, jnp.max(s, axis=0, keepdims=True))
    p = jnp.exp2(s - m_new).astype(BF16)
    vt_ones = jnp.concatenate([vt, jnp.ones((L_ROWS, vt.shape[1]), BF16)], axis=0)
    pv = jnp.dot(vt_ones, p, preferred_element_type=F32)
    acc_ref[h] = jnp.exp2(m_old - m_new) * acc_ref[h] + pv
    m_ref[h] = m_new


def _attention_pipeline(last, heads, logits, consume, s_ref):
    assert LOOKAHEAD < heads

    def issue(c, h):
        s_ref[h] = logits(c, h)

    def issue_ahead(c, h, is_last):
        if h + LOOKAHEAD < heads:
            issue(c, h + LOOKAHEAD)
        elif not is_last:
            issue(c + 1, h + LOOKAHEAD - heads)

    def step(c, is_last):
        for h in range(0, heads, 2):
            issue_ahead(c, h, is_last)
            issue_ahead(c, h + 1, is_last)
            consume(c, h, s_ref[h], is_last)
            consume(c, h + 1, s_ref[h + 1], is_last)

    def body(c, carry):
        step(c, False)
        return carry

    for h in range(LOOKAHEAD):
        issue(0, h)
    lax.fori_loop(0, last, body, 0)
    step(last, True)


def _softmax_init(m_ref, acc_ref):
    m_ref[...] = jnp.full(m_ref.shape, NEG, F32)
    acc_ref[...] = jnp.zeros(acc_ref.shape, F32)


def _softmax_finish(acc_ref, g_ref, o_ref, heads, dv):
    for h in range(heads):
        acc = acc_ref[h]
        o = acc[:dv] * (1.0 / acc[dv:dv + 1])
        o_ref[h * dv:(h + 1) * dv, :] = (o * g_ref[h * dv:(h + 1) * dv, :]).astype(o_ref.dtype)


def _dsa_kernel(iq_ref, iwT_ref, ik_ref, qa_ref, ka_ref, vaT_ref, g_ref, o_ref,
                hi_ref, lo_ref, bk_ref, bias_ref, m_ref, acc_ref, s_ref, *, k_top):
    C = CHUNK
    i = pl.program_id(1)
    nk = i + 1
    row = lax.broadcasted_iota(I32, (C, C), 0)
    col = lax.broadcasted_iota(I32, (C, C), 1)
    w = iwT_ref[...]

    def chunk_rows(c):
        return pl.ds(pl.multiple_of(c * C, C), C)

    def idx_logits(c, h):
        return _nt(ik_ref[chunk_rows(c), :], iq_ref[:, h * 2 * LANES:(h + 1) * 2 * LANES])

    score = [None]

    def add_head(c, h, lg, is_last):
        term = w[h:h + 1, :] * jnp.maximum(lg, 0.0)
        score[0] = term if h == 0 else score[0] + term
        if h < IDX_HEADS - 1:
            return
        acc = score[0] + 0.0
        bits = lax.bitcast_convert_type(acc, I32)
        key = jnp.where(bits < 0, bits ^ jnp.int32(0x7FFFFFFF), bits)
        valid = (c * C + row) <= (i * C + col)
        key = jnp.where(valid, key, jnp.int32(INT_MIN))
        hi_ref[chunk_rows(c), :] = (key >> 16).astype(I16)
        lo_ref[chunk_rows(c), :] = (key ^ jnp.int32(0x8000)).astype(I16)

    _attention_pipeline(i, IDX_HEADS, idx_logits, add_head, s_ref)

    @pl.when(nk % 2 == 1)
    def _():
        hi_ref[chunk_rows(nk), :] = jnp.full((C, C), MIN16, I16)
        lo_ref[chunk_rows(nk), :] = jnp.full((C, C), MIN16, I16)

    n_pairs = (nk + 1) // 2

    def pair_rows(d):
        return pl.ds(pl.multiple_of(d * 2 * C, 2 * C), 2 * C)

    def pack16(v):
        return jnp.broadcast_to(v, (PACK, C)).astype(I16)

    def tiles(x):
        return [x[j * PACK:(j + 1) * PACK] for j in range(x.shape[0] // PACK)]

    def tree_sum(parts):
        while len(parts) > 1:
            parts = [parts[j] + parts[j + 1] for j in range(0, len(parts), 2)]
        return parts[0]

    def count16(ref, pred):
        def body(d, acc):
            hits = [jnp.where(pred(t), jnp.int16(1), jnp.int16(0)) for t in tiles(ref[pair_rows(d), :])]
            return acc + tree_sum(hits)
        acc = lax.fori_loop(0, n_pairs, body, jnp.zeros((PACK, C), I16))
        return jnp.sum(acc.astype(I32), axis=0, keepdims=True)

    def kth_largest(ref, k):
        def bit_body(b, v):
            cand = v + jnp.left_shift(jnp.int32(1), 15 - b)
            c16 = pack16(cand)
            return jnp.where(count16(ref, lambda t: t >= c16) >= k, cand, v)
        return lax.fori_loop(0, 16, bit_body, jnp.full((1, C), MIN16, I32))

    thr_hi = kth_largest(hi_ref, k_top)
    hi16 = pack16(thr_hi)

    def bucket_pair(d, acc):
        his, los = tiles(hi_ref[pair_rows(d), :]), tiles(lo_ref[pair_rows(d), :])
        for j, (th, tl) in enumerate(zip(his, los)):
            bk_ref[pl.ds(pl.multiple_of(d * 2 * C, 2 * C) + j * PACK, PACK), :] = jnp.where(
                th == hi16, tl, jnp.int16(MIN16))
        return acc + tree_sum([jnp.where(th > hi16, jnp.int16(1), jnp.int16(0)) for th in his])

    above = lax.fori_loop(0, n_pairs, bucket_pair, jnp.zeros((PACK, C), I16))
    rank = k_top - jnp.sum(above.astype(I32), axis=0, keepdims=True)
    thr_lo = kth_largest(bk_ref, rank)
    lo16 = pack16(thr_lo)

    need = (rank - count16(bk_ref, lambda t: t > lo16)).astype(F32)
    tri = (row >= col).astype(BF16)

    def threshold_masks(c):
        hi = hi_ref[chunk_rows(c), :].astype(I32)
        lo = lo_ref[chunk_rows(c), :].astype(I32)
        same_hi = hi == thr_hi
        return same_hi & (lo == thr_lo), (hi > thr_hi) | (same_hi & (lo > thr_lo))

    def bias_pair(d, carry):
        chunks = (2 * d, 2 * d + 1)
        masks = [threshold_masks(c) for c in chunks]
        prefixes = [jnp.dot(tri, eq.astype(BF16), preferred_element_type=F32) for eq, _ in masks]
        for c, (eq, gt), prefix in zip(chunks, masks, prefixes):
            sel = gt | (eq & (prefix + carry <= need))
            valid = (c * C + row) <= (i * C + col)
            bias_ref[chunk_rows(c), :] = jnp.where(sel & valid, 0.0, NEG)
            carry = carry + jnp.sum(eq.astype(F32), axis=0, keepdims=True)
        return carry

    lax.fori_loop(0, n_pairs, bias_pair, jnp.zeros((1, C), F32))

    _softmax_init(m_ref, acc_ref)

    def logits(c, h):
        s = _nt(ka_ref[chunk_rows(c), (h // 2) * LANES:(h // 2 + 1) * LANES],
                qa_ref[:, h * LANES:(h + 1) * LANES])
        return s + bias_ref[chunk_rows(c), :]

    def consume(c, h, s, is_last):
        _softmax_step(s, vaT_ref[c, h * A_HEAD_DIM:(h + 1) * A_HEAD_DIM, :], m_ref, acc_ref, h)

    _attention_pipeline(i, A_HEADS, logits, consume, s_ref)
    _softmax_finish(acc_ref, g_ref, o_ref, A_HEADS, A_HEAD_DIM)


def _dsa_attention(iq, iwT, ik, qa, ka, vaT, gT, batch, seq):
    C = CHUNK
    nc = seq // C
    k_top = min(TOPK_MAX, seq // 4)
    sel_rows = (nc + nc % 2) * C
    return pl.pallas_call(
        functools.partial(_dsa_kernel, k_top=k_top),
        grid=(batch, nc),
        in_specs=[
            pl.BlockSpec((C, IDX_HEADS * 2 * LANES), lambda b, i: (b * nc + i, 0)),
            pl.BlockSpec((None, 16, C), lambda b, i: (b * nc + i, 0, 0)),
            pl.BlockSpec((seq, 2 * LANES), lambda b, i: (b, 0)),
            pl.BlockSpec((C, A_HEADS * LANES), lambda b, i: (b * nc + i, 0)),
            pl.BlockSpec((seq, A_WIDTH), lambda b, i: (b, 0)),
            pl.BlockSpec((nc, A_WIDTH, C), lambda b, i: (b, 0, 0)),
            pl.BlockSpec((None, A_WIDTH, C), lambda b, i: (b * nc + i, 0, 0)),
        ],
        out_specs=pl.BlockSpec((None, A_WIDTH, C), lambda b, i: (b * nc + i, 0, 0)),
        out_shape=jax.ShapeDtypeStruct((batch * nc, A_WIDTH, C), BF16),
        scratch_shapes=[
            pltpu.VMEM((sel_rows, C), I16),
            pltpu.VMEM((sel_rows, C), I16),
            pltpu.VMEM((sel_rows, C), I16),
            pltpu.VMEM((sel_rows, C), F32),
            pltpu.VMEM((A_HEADS, 1, C), F32),
            pltpu.VMEM((A_HEADS, A_HEAD_DIM + L_ROWS, C), F32),
            pltpu.VMEM((A_HEADS, C, C), F32),
        ],
        compiler_params=pltpu.CompilerParams(dimension_semantics=("parallel", "arbitrary"),
                                             vmem_limit_bytes=VMEM_LIMIT),
        name="dsa_attention",
    )(iq, iwT, ik, qa, ka, vaT, gT)


def _mla_kernel(qb_ref, kb_ref, vbT_ref, g_ref, o_ref, m_ref, acc_ref, s_ref):
    C = CHUNK
    i = pl.program_id(1)
    row = lax.broadcasted_iota(I32, (C, C), 0)
    col = lax.broadcasted_iota(I32, (C, C), 1)
    causal = row <= col
    _softmax_init(m_ref, acc_ref)

    def logits(c, h):
        kc = kb_ref[pl.ds(pl.multiple_of(c * C, C), C), h * LANES:(h + 1) * LANES]
        return _nt(kc, qb_ref[:, h * LANES:(h + 1) * LANES])

    def consume(c, h, s, is_last):
        if is_last:
            s = jnp.where(causal, s, NEG)
        _softmax_step(s, vbT_ref[c, h * B_V_DIM:(h + 1) * B_V_DIM, :], m_ref, acc_ref, h)

    _attention_pipeline(i, B_HEADS, logits, consume, s_ref)
    _softmax_finish(acc_ref, g_ref, o_ref, B_HEADS, B_V_DIM)


def _mla_attention(qb, kb, vbT, gT, batch, seq):
    C = CHUNK
    nc = seq // C
    return pl.pallas_call(
        _mla_kernel,
        grid=(batch, nc),
        in_specs=[
            pl.BlockSpec((C, B_HEADS * LANES), lambda b, i: (b * nc + i, 0)),
            pl.BlockSpec((seq, B_HEADS * LANES), lambda b, i: (b, 0)),
            pl.BlockSpec((nc, B_WIDTH, C), lambda b, i: (b, 0, 0)),
            pl.BlockSpec((None, B_WIDTH, C), lambda b, i: (b * nc + i, 1, 0)),
        ],
        out_specs=pl.BlockSpec((None, B_WIDTH, C), lambda b, i: (b * nc + i, 0, 0)),
        out_shape=jax.ShapeDtypeStruct((batch * nc, B_WIDTH, C), BF16),
        scratch_shapes=[
            pltpu.VMEM((B_HEADS, 1, C), F32),
            pltpu.VMEM((B_HEADS, B_V_DIM + L_ROWS, C), F32),
            pltpu.VMEM((B_HEADS, C, C), F32),
        ],
        compiler_params=pltpu.CompilerParams(dimension_semantics=("parallel", "arbitrary"),
                                             vmem_limit_bytes=VMEM_LIMIT),
        name="mla_attention",
    )(qb, kb, vbT, gT)


def _out_kernel(oa_ref, ob_ref, x_ref, w_ref, lng_ref, lnb_ref, y_ref, *, alpha):
    out = (lax.dot_general(oa_ref[...], w_ref[:A_WIDTH, :], TN_DIMS, preferred_element_type=F32)
           + lax.dot_general(ob_ref[...], w_ref[A_WIDTH:, :], TN_DIMS, preferred_element_type=F32))
    z = alpha * x_ref[...] + out
    mu = jnp.mean(z, axis=-1, keepdims=True)
    zc = z - mu
    var = jnp.mean(zc * zc, axis=-1, keepdims=True)
    y_ref[...] = zc * lax.rsqrt(var + LN_EPS) * lng_ref[...] + lnb_ref[...]


def _output(oaT, obT, x2, w_out, ln_g, ln_b, alpha):
    C = CHUNK
    n = x2.shape[0]
    const = lambda a: pl.BlockSpec(a.shape, lambda t: (0,) * a.ndim)
    return pl.pallas_call(
        functools.partial(_out_kernel, alpha=alpha),
        grid=(n // C,),
        in_specs=[
            pl.BlockSpec((None, A_WIDTH, C), lambda t: (t, 0, 0)),
            pl.BlockSpec((None, B_WIDTH, C), lambda t: (t, 0, 0)),
            pl.BlockSpec((C, D_MODEL), lambda t: (t, 0)),
            const(w_out), const(ln_g), const(ln_b),
        ],
        out_specs=pl.BlockSpec((C, D_MODEL), lambda t: (t, 0)),
        out_shape=jax.ShapeDtypeStruct((n, D_MODEL), F32),
        compiler_params=pltpu.CompilerParams(dimension_semantics=("parallel",),
                                             vmem_limit_bytes=VMEM_LIMIT),
        name="output",
    )(oaT, obT, x2, w_out, ln_g, ln_b)


def _prepare_weights(w_in, q_norm_g, w_uq, kv_norm_g, w_ukv):
    offs = np.concatenate([[0], np.cumsum(IN_SPLITS)])
    (wqa, wka, wva, wga, wiq, wik, wiw, wcq, wckv, wkr, wgb) = [
        w_in[:, int(offs[j]):int(offs[j + 1])] for j in range(len(IN_SPLITS))]
    d = w_in.shape[0]
    zeros = lambda c: jnp.zeros((d, c), w_in.dtype)
    widx = jnp.concatenate([wiq] + [wik] * (LANES // IDX_DIM) + [wiw, zeros(LANES - IDX_HEADS)], axis=1)
    wkpe = jnp.concatenate([zeros(B_NOPE_DIM), wkr, zeros(LANES - B_NOPE_DIM - B_ROPE_DIM)], axis=1)
    wgT = jnp.concatenate([wga, wgb], axis=1).T

    qk = B_NOPE_DIM + B_ROPE_DIM
    wuq = w_uq.reshape(Q_LORA, B_HEADS, qk)
    wuq = jnp.pad(wuq, ((0, 0), (0, 0), (0, LANES - qk))).reshape(Q_LORA, B_HEADS * LANES)
    wukv = w_ukv.reshape(KV_LORA, B_HEADS, B_NOPE_DIM + B_V_DIM)
    wukvk = jnp.pad(wukv[:, :, :B_NOPE_DIM], ((0, 0), (0, 0), (0, LANES - B_NOPE_DIM)))
    wukvk = wukvk.reshape(KV_LORA, B_HEADS * LANES)
    wukvvT = wukv[:, :, B_NOPE_DIM:].reshape(KV_LORA, B_WIDTH).T

    bf = lambda a: a.astype(BF16)
    order = (np.arange(A_HEADS)[:, None] * A_HEAD_DIM + _head64_order()[None, :]).reshape(-1)
    return [bf(wqa[:, order]), bf(wka[:, order]), widx.astype(F32), bf(wkpe), bf(wcq), bf(wckv),
            bf(wva.T), bf(wgT),
            q_norm_g.reshape(1, Q_LORA).astype(F32), kv_norm_g.reshape(1, KV_LORA).astype(F32),
            bf(wuq), bf(wukvk), bf(wukvvT)]


def kernel(x, positions, w_in, q_norm_g, w_uq, kv_norm_g, w_ukv, w_out, ln_g, ln_b):
    batch, seq, d_model = x.shape
    depth = w_in.shape[0]
    alpha = (2 * depth) ** 0.25
    assert d_model == D_MODEL and seq % ROW_TILE == 0 and ROW_TILE % CHUNK == 0
    n = batch * seq
    pos_col = positions.reshape(n, 1).astype(I32)
    x2 = x.reshape(n, d_model)
    for l in range(depth):
        weights = _prepare_weights(w_in[l], q_norm_g[l], w_uq[l], kv_norm_g[l], w_ukv[l])
        qa, ka, iq, ik, qb, kb, vaT, vbT, gT, iwT = _projection(x2, pos_col, weights)
        oaT = _dsa_attention(iq, iwT, ik, qa, ka, vaT, gT, batch, seq)
        obT = _mla_attention(qb, kb, vbT, gT, batch, seq)
        x2 = _output(oaT, obT, x2, w_out[l].astype(BF16),
                     ln_g[l].reshape(1, d_model), ln_b[l].reshape(1, d_model), alpha)
    return x2.reshape(batch, seq, d_model)
```

```python
import functools

import jax
import jax.numpy as jnp
import numpy as np
from jax import lax
from jax.experimental import pallas as pl
from jax.experimental.pallas import tpu as pltpu

F32 = jnp.float32
BF16 = jnp.bfloat16
I32 = jnp.int32
I16 = jnp.int16

D_MODEL = 1024
A_HEADS = 8
A_HEAD_DIM = 64
A_WIDTH = A_HEADS * A_HEAD_DIM
IDX_HEADS = 8
IDX_DIM = 32
TOPK_MAX = 256
B_HEADS = 8
B_NOPE_DIM = 64
B_ROPE_DIM = 32
B_V_DIM = 64
B_WIDTH = B_HEADS * B_V_DIM
Q_LORA = 256
KV_LORA = 128
ROPE_THETA = 10000.0
LN_EPS = 1e-5
RMS_EPS = 1e-6

IN_SPLITS = (A_WIDTH, A_WIDTH, A_WIDTH, A_WIDTH, IDX_HEADS * IDX_DIM, IDX_DIM, IDX_HEADS,
             Q_LORA, KV_LORA, B_ROPE_DIM, B_WIDTH)

LANES = 128
CHUNK = 256
ROW_TILE = 512
VMEM_LIMIT = 56 * 1024 * 1024
NEG = -1e30
INT_MIN = -2 ** 31
MIN16 = -2 ** 15
PACK = 16
LOG2E = 1.4426950408889634
L_ROWS = 16
LOOKAHEAD = 4

NT_DIMS = (((1,), (1,)), ((), ()))
TN_DIMS = (((0,), (0,)), ((), ()))


def _nt(a, b):
    return lax.dot_general(a, b, NT_DIMS, preferred_element_type=F32)


FREQ_BITS = 12


ROPE_HALF = IDX_DIM // 2


def _head64_order():
    i = np.arange(ROPE_HALF)
    half = A_HEAD_DIM // 2
    return np.concatenate([2 * i, half + 2 * i, 2 * i + 1, half + 2 * i + 1])


def _rope_patterns():
    lane = np.arange(LANES)
    freq_index = 2 * (lane % ROPE_HALF) + (lane // IDX_DIM) % 2
    rest = ROPE_THETA ** (-2.0 * freq_index.astype(np.float64) / A_HEAD_DIM)
    rows = []
    for _ in range(2):
        mant, exp = np.frexp(rest)
        piece = np.ldexp(np.floor(mant * 2 ** FREQ_BITS) / 2 ** FREQ_BITS, exp)
        rows.append(piece)
        rest = rest - piece
    rows.append(rest)
    rows.append(np.where(lane % IDX_DIM < ROPE_HALF, -1.0, 1.0))
    return jnp.asarray(np.stack(rows), F32)


def _rope_tables(pos_ref, pat_ref, lane):
    pos = pos_ref[...].astype(F32)
    a0 = pos * pat_ref[0:1, :]
    dl = pos * pat_ref[1:2, :] + pos * pat_ref[2:3, :]
    c0, s0, cd, sd = jnp.cos(a0), jnp.sin(a0), jnp.cos(dl), jnp.sin(dl)
    c64 = c0 * cd - s0 * sd
    s64 = (s0 * cd + c0 * sd) * pat_ref[3:4, :]
    odd_block = (lane // IDX_DIM) % 2 == 1
    c32 = jnp.where(odd_block, pltpu.roll(c64, IDX_DIM, 1), c64)
    s32 = jnp.where(odd_block, pltpu.roll(s64, IDX_DIM, 1), s64)
    return c64, s64, c32, s32


def _rope_group(xg, cos, sin_signed, low):
    swapped = jnp.where(low, pltpu.roll(xg, LANES - ROPE_HALF, 1), pltpu.roll(xg, ROPE_HALF, 1))
    return xg * cos + swapped * sin_signed


def _proj_kernel(x_ref, pos_ref, pat_ref,
                 wq_ref, wk_ref, widx_ref, wkpe_ref, wcq_ref, wckv_ref,
                 wvT_ref, wgT_ref, gq_ref, gkv_ref, wuq_ref, wukvk_ref, wukvvT_ref,
                 qa_ref, ka_ref, iq_ref, ik_ref, qb_ref, kb_ref,
                 vaT_ref, vbT_ref, gT_ref, iwT_ref):
    tm = x_ref.shape[0]
    xb = x_ref[...].astype(BF16)
    lane = lax.broadcasted_iota(I32, (tm, LANES), 1)
    c64, s64, c32, s32 = _rope_tables(pos_ref, pat_ref, lane)
    low = (lane % IDX_DIM) < ROPE_HALF

    def dot(w_ref):
        return jnp.dot(xb, w_ref[...], preferred_element_type=F32)

    def rope_all(h, cos, sin, out_ref, scale=None, head_dim=None):
        for g in range(h.shape[1] // LANES):
            r = _rope_group(h[:, g * LANES:(g + 1) * LANES], cos, sin, low)
            if scale is not None:
                r = r * scale
            if head_dim is None:
                out_ref[:, g * LANES:(g + 1) * LANES] = r.astype(out_ref.dtype)
                continue
            per_group = LANES // head_dim
            for j in range(per_group):
                o = (g * per_group + j) * LANES
                out_ref[:, o:o + LANES] = jnp.where(lane // head_dim == j, r, 0.0).astype(out_ref.dtype)

    rope_all(dot(wq_ref), c64, s64, qa_ref, A_HEAD_DIM ** -0.5 * LOG2E,
             head_dim=A_HEAD_DIM)
    rope_all(dot(wk_ref), c64, s64, ka_ref)

    hidx = jnp.dot(x_ref[...], widx_ref[...], precision=lax.Precision.HIGHEST,
                   preferred_element_type=F32)
    slot = lane // IDX_DIM

    def split3(v):
        hi = v.astype(BF16).astype(F32)
        mid = (v - hi).astype(BF16).astype(F32)
        lo = (v - hi - mid).astype(BF16).astype(F32)
        return hi, mid, lo

    def idx_rope(g):
        return _rope_group(hidx[:, g * LANES:(g + 1) * LANES], c32, s32, low)

    k_hi, k_mid, k_lo = split3(idx_rope(2))
    ik_ref[:, :LANES] = jnp.where(slot == 1, k_mid, jnp.where(slot == 3, k_lo, k_hi)).astype(BF16)
    ik_ref[:, LANES:] = jnp.where(slot == 0, k_hi, jnp.where(slot == 1, k_mid, 0.0)).astype(BF16)
    slots = LANES // IDX_DIM
    for g in range(IDX_HEADS // slots):
        rolled = [[p if d == 0 else pltpu.roll(p, d * IDX_DIM, 1) for d in range(slots)]
                  for p in split3(idx_rope(g))]
        for s_ in range(slots):
            at = lambda piece, t: rolled[piece][(t - s_) % slots]
            b1 = jnp.where(slot == 0, at(0, 0), jnp.where(slot == 1, at(0, 1),
                           jnp.where(slot == 2, at(1, 2), at(0, 3))))
            b2 = jnp.where(slot == 0, at(2, 0), jnp.where(slot == 1, at(1, 1), 0.0))
            o = (g * slots + s_) * 2 * LANES
            iq_ref[:, o:o + LANES] = b1.astype(BF16)
            iq_ref[:, o + LANES:o + 2 * LANES] = b2.astype(BF16)

    def rmsnorm(c, g_ref):
        ms = jnp.mean(c * c, axis=-1, keepdims=True)
        return (c * lax.rsqrt(ms + RMS_EPS) * g_ref[...]).astype(BF16)

    cqn = rmsnorm(dot(wcq_ref), gq_ref)
    qb = jnp.dot(cqn, wuq_ref[...], preferred_element_type=F32)
    pe_lane = (lane >= B_NOPE_DIM) & (lane < B_NOPE_DIM + B_ROPE_DIM)
    b_scale = (B_NOPE_DIM + B_ROPE_DIM) ** -0.5 * LOG2E
    for h in range(B_HEADS):
        g = qb[:, h * LANES:(h + 1) * LANES]
        r = jnp.where(pe_lane, _rope_group(g, c32, s32, low), g)
        qb_ref[:, h * LANES:(h + 1) * LANES] = (r * b_scale).astype(BF16)

    ckvn = rmsnorm(dot(wckv_ref), gkv_ref)
    kpe = _rope_group(dot(wkpe_ref), c32, s32, low)
    kn = jnp.dot(ckvn, wukvk_ref[...], preferred_element_type=F32)
    for h in range(B_HEADS):
        kb_ref[:, h * LANES:(h + 1) * LANES] = (kn[:, h * LANES:(h + 1) * LANES] + kpe).astype(BF16)

    def store_t(val, out_ref):
        for j in range(tm // CHUNK):
            out_ref[j] = val[:, j * CHUNK:(j + 1) * CHUNK].astype(out_ref.dtype)

    store_t(_nt(wukvvT_ref[...], ckvn), vbT_ref)
    store_t(_nt(wvT_ref[...], xb), vaT_ref)
    gt = _nt(wgT_ref[...], xb)
    store_t(gt * jax.nn.sigmoid(gt), gT_ref)
    iw_t = jnp.transpose(hidx[:, 3 * LANES:4 * LANES])
    store_t(iw_t[:16] * (IDX_DIM ** -0.5 * IDX_HEADS ** -0.5), iwT_ref)


def _projection(x2, pos_col, weights):
    n = x2.shape[0]
    tm = ROW_TILE
    nchunks = n // CHUNK
    cpt = tm // CHUNK

    def rows(width):
        return pl.BlockSpec((tm, width), lambda t: (t, 0))

    def full(a):
        return pl.BlockSpec(a.shape, lambda t: (0,) * a.ndim)

    def tspec(feat):
        return pl.BlockSpec((cpt, feat, CHUNK), lambda t: (t, 0, 0))

    out_shape = [
        jax.ShapeDtypeStruct((n, A_HEADS * LANES), BF16),
        jax.ShapeDtypeStruct((n, A_WIDTH), BF16),
        jax.ShapeDtypeStruct((n, IDX_HEADS * 2 * LANES), BF16),
        jax.ShapeDtypeStruct((n, 2 * LANES), BF16),
        jax.ShapeDtypeStruct((n, B_HEADS * LANES), BF16),
        jax.ShapeDtypeStruct((n, B_HEADS * LANES), BF16),
        jax.ShapeDtypeStruct((nchunks, A_WIDTH, CHUNK), BF16),
        jax.ShapeDtypeStruct((nchunks, B_WIDTH, CHUNK), BF16),
        jax.ShapeDtypeStruct((nchunks, A_WIDTH + B_WIDTH, CHUNK), F32),
        jax.ShapeDtypeStruct((nchunks, 16, CHUNK), F32),
    ]
    out_specs = [rows(A_HEADS * LANES), rows(A_WIDTH), rows(IDX_HEADS * 2 * LANES), rows(2 * LANES),
                 rows(B_HEADS * LANES), rows(B_HEADS * LANES),
                 tspec(A_WIDTH), tspec(B_WIDTH), tspec(A_WIDTH + B_WIDTH), tspec(16)]
    patterns = _rope_patterns()
    in_specs = [rows(D_MODEL), rows(1), full(patterns)] + [full(w) for w in weights]
    return pl.pallas_call(
        _proj_kernel,
        grid=(n // tm,),
        in_specs=in_specs,
        out_specs=out_specs,
        out_shape=out_shape,
        compiler_params=pltpu.CompilerParams(dimension_semantics=("parallel",),
                                             vmem_limit_bytes=VMEM_LIMIT),
        name="projection",
    )(x2, pos_col, patterns, *weights)


def _softmax_step(s, smax, vt, m_ref, acc_ref, h):
    m_old = m_ref[h]
    m_new = jnp.maximum(m_old, smax)
    p = jnp.exp2(s - m_new).astype(BF16)
    vt_ones = jnp.concatenate([vt, jnp.ones((L_ROWS, vt.shape[1]), BF16)], axis=0)
    pv = jnp.dot(vt_ones, p, preferred_element_type=F32)
    acc_ref[h] = jnp.exp2(m_old - m_new) * acc_ref[h] + pv
    m_ref[h] = m_new


def _attention_pipeline(last, heads, logits, consume, s_ref, smax_ref):
    assert LOOKAHEAD < heads

    def issue(c, h):
        s = logits(c, h)
        s_ref[h] = s
        smax_ref[h] = jnp.max(s, axis=0, keepdims=True)

    def step(c, is_last):
        for h in range(heads):
            if h + LOOKAHEAD < heads:
                issue(c, h + LOOKAHEAD)
            elif not is_last:
                issue(c + 1, h + LOOKAHEAD - heads)
            consume(c, h, s_ref[h], smax_ref[h], is_last)

    def body(c, carry):
        step(c, False)
        return carry

    for h in range(LOOKAHEAD):
        issue(0, h)
    lax.fori_loop(0, last, body, 0)
    step(last, True)


def _softmax_init(m_ref, acc_ref):
    m_ref[...] = jnp.full(m_ref.shape, NEG, F32)
    acc_ref[...] = jnp.zeros(acc_ref.shape, F32)


def _softmax_finish(acc_ref, g_ref, o_ref, heads, dv):
    for h in range(heads):
        acc = acc_ref[h]
        o = acc[:dv] * (1.0 / acc[dv:dv + 1])
        o_ref[h * dv:(h + 1) * dv, :] = (o * g_ref[h * dv:(h + 1) * dv, :]).astype(o_ref.dtype)


def _dsa_kernel(iq_ref, iwT_ref, ik_ref, qa_ref, ka_ref, vaT_ref, g_ref, o_ref,
                hi_ref, lo_ref, bk_ref, bias_ref, m_ref, acc_ref, s_ref, smax_ref, *, k_top):
    C = CHUNK
    i = pl.program_id(1)
    nk = i + 1
    row = lax.broadcasted_iota(I32, (C, C), 0)
    col = lax.broadcasted_iota(I32, (C, C), 1)
    w = iwT_ref[...]

    def chunk_rows(c):
        return pl.ds(pl.multiple_of(c * C, C), C)

    def score_chunk(c, carry):
        ikc = ik_ref[chunk_rows(c), :]
        acc = jnp.zeros((C, C), F32)
        for h in range(IDX_HEADS):
            lg = _nt(ikc, iq_ref[:, h * 2 * LANES:(h + 1) * 2 * LANES])
            acc = acc + w[h:h + 1, :] * jnp.maximum(lg, 0.0)
        acc = acc + 0.0
        bits = lax.bitcast_convert_type(acc, I32)
        key = jnp.where(bits < 0, bits ^ jnp.int32(0x7FFFFFFF), bits)
        valid = (c * C + row) <= (i * C + col)
        key = jnp.where(valid, key, jnp.int32(INT_MIN))
        hi_ref[chunk_rows(c), :] = (key >> 16).astype(I16)
        lo_ref[chunk_rows(c), :] = (key ^ jnp.int32(0x8000)).astype(I16)
        return carry

    lax.fori_loop(0, nk, score_chunk, 0)

    @pl.when(nk % 2 == 1)
    def _():
        hi_ref[chunk_rows(nk), :] = jnp.full((C, C), MIN16, I16)
        lo_ref[chunk_rows(nk), :] = jnp.full((C, C), MIN16, I16)

    n_pairs = (nk + 1) // 2

    def pair_rows(d):
        return pl.ds(pl.multiple_of(d * 2 * C, 2 * C), 2 * C)

    def pack16(v):
        return jnp.broadcast_to(v, (PACK, C)).astype(I16)

    def tiles(x):
        return [x[j * PACK:(j + 1) * PACK] for j in range(x.shape[0] // PACK)]

    def tree_sum(parts):
        while len(parts) > 1:
            parts = [parts[j] + parts[j + 1] for j in range(0, len(parts), 2)]
        return parts[0]

    def count16(ref, pred):
        def body(d, acc):
            hits = [jnp.where(pred(t), jnp.int16(1), jnp.int16(0)) for t in tiles(ref[pair_rows(d), :])]
            return acc + tree_sum(hits)
        acc = lax.fori_loop(0, n_pairs, body, jnp.zeros((PACK, C), I16))
        return jnp.sum(acc.astype(I32), axis=0, keepdims=True)

    def kth_largest(ref, k):
        def bit_body(b, v):
            cand = v + jnp.left_shift(jnp.int32(1), 15 - b)
            c16 = pack16(cand)
            return jnp.where(count16(ref, lambda t: t >= c16) >= k, cand, v)
        return lax.fori_loop(0, 16, bit_body, jnp.full((1, C), MIN16, I32))

    thr_hi = kth_largest(hi_ref, k_top)
    hi16 = pack16(thr_hi)

    def bucket_pair(d, acc):
        his, los = tiles(hi_ref[pair_rows(d), :]), tiles(lo_ref[pair_rows(d), :])
        for j, (th, tl) in enumerate(zip(his, los)):
            bk_ref[pl.ds(pl.multiple_of(d * 2 * C, 2 * C) + j * PACK, PACK), :] = jnp.where(
                th == hi16, tl, jnp.int16(MIN16))
        return acc + tree_sum([jnp.where(th > hi16, jnp.int16(1), jnp.int16(0)) for th in his])

    above = lax.fori_loop(0, n_pairs, bucket_pair, jnp.zeros((PACK, C), I16))
    rank = k_top - jnp.sum(above.astype(I32), axis=0, keepdims=True)
    thr_lo = kth_largest(bk_ref, rank)
    lo16 = pack16(thr_lo)

    need = (rank - count16(bk_ref, lambda t: t > lo16)).astype(F32)
    tri = (row >= col).astype(BF16)

    def bias_chunk(c, carry):
        hi = hi_ref[chunk_rows(c), :].astype(I32)
        lo = lo_ref[chunk_rows(c), :].astype(I32)
        same_hi = hi == thr_hi
        eq = same_hi & (lo == thr_lo)
        gt = (hi > thr_hi) | (same_hi & (lo > thr_lo))
        prefix = jnp.dot(tri, eq.astype(BF16), preferred_element_type=F32) + carry
        sel = gt | (eq & (prefix <= need))
        valid = (c * C + row) <= (i * C + col)
        bias_ref[chunk_rows(c), :] = jnp.where(sel & valid, 0.0, NEG)
        return carry + jnp.sum(eq.astype(F32), axis=0, keepdims=True)

    lax.fori_loop(0, nk, bias_chunk, jnp.zeros((1, C), F32))

    _softmax_init(m_ref, acc_ref)

    def logits(c, h):
        s = _nt(ka_ref[chunk_rows(c), (h // 2) * LANES:(h // 2 + 1) * LANES],
                qa_ref[:, h * LANES:(h + 1) * LANES])
        return s + bias_ref[chunk_rows(c), :]

    def consume(c, h, s, smax, is_last):
        _softmax_step(s, smax, vaT_ref[c, h * A_HEAD_DIM:(h + 1) * A_HEAD_DIM, :], m_ref, acc_ref, h)

    _attention_pipeline(i, A_HEADS, logits, consume, s_ref, smax_ref)
    _softmax_finish(acc_ref, g_ref, o_ref, A_HEADS, A_HEAD_DIM)


def _dsa_attention(iq, iwT, ik, qa, ka, vaT, gT, batch, seq):
    C = CHUNK
    nc = seq // C
    k_top = min(TOPK_MAX, seq // 4)
    sel_rows = (nc + nc % 2) * C
    return pl.pallas_call(
        functools.partial(_dsa_kernel, k_top=k_top),
        grid=(batch, nc),
        in_specs=[
            pl.BlockSpec((C, IDX_HEADS * 2 * LANES), lambda b, i: (b * nc + i, 0)),
            pl.BlockSpec((None, 16, C), lambda b, i: (b * nc + i, 0, 0)),
            pl.BlockSpec((seq, 2 * LANES), lambda b, i: (b, 0)),
            pl.BlockSpec((C, A_HEADS * LANES), lambda b, i: (b * nc + i, 0)),
            pl.BlockSpec((seq, A_WIDTH), lambda b, i: (b, 0)),
            pl.BlockSpec((nc, A_WIDTH, C), lambda b, i: (b, 0, 0)),
            pl.BlockSpec((None, A_WIDTH, C), lambda b, i: (b * nc + i, 0, 0)),
        ],
        out_specs=pl.BlockSpec((None, A_WIDTH, C), lambda b, i: (b * nc + i, 0, 0)),
        out_shape=jax.ShapeDtypeStruct((batch * nc, A_WIDTH, C), BF16),
        scratch_shapes=[
            pltpu.VMEM((sel_rows, C), I16),
            pltpu.VMEM((sel_rows, C), I16),
            pltpu.VMEM((sel_rows, C), I16),
            pltpu.VMEM((seq, C), F32),
            pltpu.VMEM((A_HEADS, 1, C), F32),
            pltpu.VMEM((A_HEADS, A_HEAD_DIM + L_ROWS, C), F32),
            pltpu.VMEM((A_HEADS, C, C), F32),
            pltpu.VMEM((A_HEADS, 1, C), F32),
        ],
        compiler_params=pltpu.CompilerParams(dimension_semantics=("parallel", "arbitrary"),
                                             vmem_limit_bytes=VMEM_LIMIT),
        name="dsa_attention",
    )(iq, iwT, ik, qa, ka, vaT, gT)


def _mla_kernel(qb_ref, kb_ref, vbT_ref, g_ref, o_ref, m_ref, acc_ref, s_ref, smax_ref):
    C = CHUNK
    i = pl.program_id(1)
    row = lax.broadcasted_iota(I32, (C, C), 0)
    col = lax.broadcasted_iota(I32, (C, C), 1)
    causal = row <= col
    _softmax_init(m_ref, acc_ref)

    def logits(c, h):
        kc = kb_ref[pl.ds(pl.multiple_of(c * C, C), C), h * LANES:(h + 1) * LANES]
        return _nt(kc, qb_ref[:, h * LANES:(h + 1) * LANES])

    def consume(c, h, s, smax, is_last):
        if is_last:
            s = jnp.where(causal, s, NEG)
            smax = jnp.max(s, axis=0, keepdims=True)
        _softmax_step(s, smax, vbT_ref[c, h * B_V_DIM:(h + 1) * B_V_DIM, :], m_ref, acc_ref, h)

    _attention_pipeline(i, B_HEADS, logits, consume, s_ref, smax_ref)
    _softmax_finish(acc_ref, g_ref, o_ref, B_HEADS, B_V_DIM)


def _mla_attention(qb, kb, vbT, gT, batch, seq):
    C = CHUNK
    nc = seq // C
    return pl.pallas_call(
        _mla_kernel,
        grid=(batch, nc),
        in_specs=[
            pl.BlockSpec((C, B_HEADS * LANES), lambda b, i: (b * nc + i, 0)),
            pl.BlockSpec((seq, B_HEADS * LANES), lambda b, i: (b, 0)),
            pl.BlockSpec((nc, B_WIDTH, C), lambda b, i: (b, 0, 0)),
            pl.BlockSpec((None, B_WIDTH, C), lambda b, i: (b * nc + i, 1, 0)),
        ],
        out_specs=pl.BlockSpec((None, B_WIDTH, C), lambda b, i: (b * nc + i, 0, 0)),
        out_shape=jax.ShapeDtypeStruct((batch * nc, B_WIDTH, C), BF16),
        scratch_shapes=[
            pltpu.VMEM((B_HEADS, 1, C), F32),
            pltpu.VMEM((B_HEADS, B_V_DIM + L_ROWS, C), F32),
            pltpu.VMEM((B_HEADS, C, C), F32),
            pltpu.VMEM((B_HEADS, 1, C), F32),
        ],
        compiler_params=pltpu.CompilerParams(dimension_semantics=("parallel", "arbitrary"),
                                             vmem_limit_bytes=VMEM_LIMIT),
        name="mla_attention",
    )(qb, kb, vbT, gT)


def _out_kernel(oa_ref, ob_ref, x_ref, w_ref, lng_ref, lnb_ref, y_ref, *, alpha):
    for j in range(oa_ref.shape[0]):
        rows = slice(j * CHUNK, (j + 1) * CHUNK)
        out = (lax.dot_general(oa_ref[j], w_ref[:A_WIDTH, :], TN_DIMS, preferred_element_type=F32)
               + lax.dot_general(ob_ref[j], w_ref[A_WIDTH:, :], TN_DIMS, preferred_element_type=F32))
        z = alpha * x_ref[rows, :] + out
        mu = jnp.mean(z, axis=-1, keepdims=True)
        zc = z - mu
        var = jnp.mean(zc * zc, axis=-1, keepdims=True)
        y_ref[rows, :] = zc * lax.rsqrt(var + LN_EPS) * lng_ref[...] + lnb_ref[...]


def _output(oaT, obT, x2, w_out, ln_g, ln_b, alpha):
    C = CHUNK
    n = x2.shape[0]
    cpt = ROW_TILE // C
    const = lambda a: pl.BlockSpec(a.shape, lambda t: (0,) * a.ndim)
    return pl.pallas_call(
        functools.partial(_out_kernel, alpha=alpha),
        grid=(n // ROW_TILE,),
        in_specs=[
            pl.BlockSpec((cpt, A_WIDTH, C), lambda t: (t, 0, 0)),
            pl.BlockSpec((cpt, B_WIDTH, C), lambda t: (t, 0, 0)),
            pl.BlockSpec((ROW_TILE, D_MODEL), lambda t: (t, 0)),
            const(w_out), const(ln_g), const(ln_b),
        ],
        out_specs=pl.BlockSpec((ROW_TILE, D_MODEL), lambda t: (t, 0)),
        out_shape=jax.ShapeDtypeStruct((n, D_MODEL), F32),
        compiler_params=pltpu.CompilerParams(dimension_semantics=("parallel",),
                                             vmem_limit_bytes=VMEM_LIMIT),
        name="output",
    )(oaT, obT, x2, w_out, ln_g, ln_b)


def _prepare_weights(w_in, q_norm_g, w_uq, kv_norm_g, w_ukv):
    offs = np.concatenate([[0], np.cumsum(IN_SPLITS)])
    (wqa, wka, wva, wga, wiq, wik, wiw, wcq, wckv, wkr, wgb) = [
        w_in[:, int(offs[j]):int(offs[j + 1])] for j in range(len(IN_SPLITS))]
    d = w_in.shape[0]
    zeros = lambda c: jnp.zeros((d, c), w_in.dtype)
    widx = jnp.concatenate([wiq] + [wik] * (LANES // IDX_DIM) + [wiw, zeros(LANES - IDX_HEADS)], axis=1)
    wkpe = jnp.concatenate([zeros(B_NOPE_DIM), wkr, zeros(LANES - B_NOPE_DIM - B_ROPE_DIM)], axis=1)
    wgT = jnp.concatenate([wga, wgb], axis=1).T

    qk = B_NOPE_DIM + B_ROPE_DIM
    wuq = w_uq.reshape(Q_LORA, B_HEADS, qk)
    wuq = jnp.pad(wuq, ((0, 0), (0, 0), (0, LANES - qk))).reshape(Q_LORA, B_HEADS * LANES)
    wukv = w_ukv.reshape(KV_LORA, B_HEADS, B_NOPE_DIM + B_V_DIM)
    wukvk = jnp.pad(wukv[:, :, :B_NOPE_DIM], ((0, 0), (0, 0), (0, LANES - B_NOPE_DIM)))
    wukvk = wukvk.reshape(KV_LORA, B_HEADS * LANES)
    wukvvT = wukv[:, :, B_NOPE_DIM:].reshape(KV_LORA, B_WIDTH).T

    bf = lambda a: a.astype(BF16)
    order = (np.arange(A_HEADS)[:, None] * A_HEAD_DIM + _head64_order()[None, :]).reshape(-1)
    return [bf(wqa[:, order]), bf(wka[:, order]), widx.astype(F32), bf(wkpe), bf(wcq), bf(wckv),
            bf(wva.T), bf(wgT),
            q_norm_g.reshape(1, Q_LORA).astype(F32), kv_norm_g.reshape(1, KV_LORA).astype(F32),
            bf(wuq), bf(wukvk), bf(wukvvT)]


def kernel(x, positions, w_in, q_norm_g, w_uq, kv_norm_g, w_ukv, w_out, ln_g, ln_b):
    batch, seq, d_model = x.shape
    depth = w_in.shape[0]
    alpha = (2 * depth) ** 0.25
    assert d_model == D_MODEL and seq % ROW_TILE == 0 and ROW_TILE % CHUNK == 0
    n = batch * seq
    pos_col = positions.reshape(n, 1).astype(I32)
    x2 = x.reshape(n, d_model)
    for l in range(depth):
        weights = _prepare_weights(w_in[l], q_norm_g[l], w_uq[l], kv_norm_g[l], w_ukv[l])
        qa, ka, iq, ik, qb, kb, vaT, vbT, gT, iwT = _projection(x2, pos_col, weights)
        oaT = _dsa_attention(iq, iwT, ik, qa, ka, vaT, gT, batch, seq)
        obT = _mla_attention(qb, kb, vbT, gT, batch, seq)
        x2 = _output(oaT, obT, x2, w_out[l].astype(BF16),
                     ln_g[l].reshape(1, d_model), ln_b[l].reshape(1, d_model), alpha)
    return x2.reshape(batch, seq, d_model)
```

```python
import functools

import jax
import jax.numpy as jnp
import numpy as np
from jax import lax
from jax.experimental import pallas as pl
from jax.experimental.pallas import tpu as pltpu

F32 = jnp.float32
BF16 = jnp.bfloat16
I32 = jnp.int32
I16 = jnp.int16

D_MODEL = 1024
A_HEADS = 8
A_HEAD_DIM = 64
A_WIDTH = A_HEADS * A_HEAD_DIM
IDX_HEADS = 8
IDX_DIM = 32
TOPK_MAX = 256
B_HEADS = 8
B_NOPE_DIM = 64
B_ROPE_DIM = 32
B_V_DIM = 64
B_WIDTH = B_HEADS * B_V_DIM
Q_LORA = 256
KV_LORA = 128
ROPE_THETA = 10000.0
LN_EPS = 1e-5
RMS_EPS = 1e-6

IN_SPLITS = (A_WIDTH, A_WIDTH, A_WIDTH, A_WIDTH, IDX_HEADS * IDX_DIM, IDX_DIM, IDX_HEADS,
             Q_LORA, KV_LORA, B_ROPE_DIM, B_WIDTH)

LANES = 128
CHUNK = 256
ROW_TILE = 512
VMEM_LIMIT = 56 * 1024 * 1024
NEG = -2.0 ** 100
INT_MIN = -2 ** 31
MIN16 = -2 ** 15
PACK = 16
LOG2E = 1.4426950408889634
L_ROWS = 16
LOOKAHEAD = 4

NT_DIMS = (((1,), (1,)), ((), ()))
TN_DIMS = (((0,), (0,)), ((), ()))


def _nt(a, b):
    return lax.dot_general(a, b, NT_DIMS, preferred_element_type=F32)


FREQ_BITS = 12


ROPE_HALF = IDX_DIM // 2


def _head64_order():
    i = np.arange(ROPE_HALF)
    half = A_HEAD_DIM // 2
    return np.concatenate([2 * i, half + 2 * i, 2 * i + 1, half + 2 * i + 1])


def _rope_patterns():
    lane = np.arange(LANES)
    freq_index = 2 * (lane % ROPE_HALF) + (lane // IDX_DIM) % 2
    rest = ROPE_THETA ** (-2.0 * freq_index.astype(np.float64) / A_HEAD_DIM)
    rows = []
    for _ in range(2):
        mant, exp = np.frexp(rest)
        piece = np.ldexp(np.floor(mant * 2 ** FREQ_BITS) / 2 ** FREQ_BITS, exp)
        rows.append(piece)
        rest = rest - piece
    rows.append(rest)
    rows.append(np.where(lane % IDX_DIM < ROPE_HALF, -1.0, 1.0))
    return jnp.asarray(np.stack(rows), F32)


def _rope_tables(pos_ref, pat_ref, lane):
    pos = pos_ref[...].astype(F32)
    a0 = pos * pat_ref[0:1, :]
    dl = pos * pat_ref[1:2, :] + pos * pat_ref[2:3, :]
    c0, s0, cd, sd = jnp.cos(a0), jnp.sin(a0), jnp.cos(dl), jnp.sin(dl)
    c64 = c0 * cd - s0 * sd
    s64 = (s0 * cd + c0 * sd) * pat_ref[3:4, :]
    odd_block = (lane // IDX_DIM) % 2 == 1
    c32 = jnp.where(odd_block, pltpu.roll(c64, IDX_DIM, 1), c64)
    s32 = jnp.where(odd_block, pltpu.roll(s64, IDX_DIM, 1), s64)
    return c64, s64, c32, s32


def _rope_group(xg, cos, sin_signed, low):
    swapped = jnp.where(low, pltpu.roll(xg, LANES - ROPE_HALF, 1), pltpu.roll(xg, ROPE_HALF, 1))
    return xg * cos + swapped * sin_signed


def _proj_kernel(x_ref, pos_ref, pat_ref,
                 wq_ref, wk_ref, widx_ref, wkpe_ref, wcq_ref, wckv_ref,
                 wvT_ref, wgT_ref, gq_ref, gkv_ref, wuq_ref, wukvk_ref, wukvvT_ref,
                 qa_ref, ka_ref, iq_ref, ik_ref, qb_ref, kb_ref,
                 vaT_ref, vbT_ref, gT_ref, iwT_ref):
    tm = x_ref.shape[0]
    xb = x_ref[...].astype(BF16)
    lane = lax.broadcasted_iota(I32, (tm, LANES), 1)
    c64, s64, c32, s32 = _rope_tables(pos_ref, pat_ref, lane)
    low = (lane % IDX_DIM) < ROPE_HALF

    def dot(w_ref):
        return jnp.dot(xb, w_ref[...], preferred_element_type=F32)

    def rope_all(h, cos, sin, out_ref, scale=None, head_dim=None):
        for g in range(h.shape[1] // LANES):
            r = _rope_group(h[:, g * LANES:(g + 1) * LANES], cos, sin, low)
            if scale is not None:
                r = r * scale
            if head_dim is None:
                out_ref[:, g * LANES:(g + 1) * LANES] = r.astype(out_ref.dtype)
                continue
            per_group = LANES // head_dim
            for j in range(per_group):
                o = (g * per_group + j) * LANES
                out_ref[:, o:o + LANES] = jnp.where(lane // head_dim == j, r, 0.0).astype(out_ref.dtype)

    rope_all(dot(wq_ref), c64, s64, qa_ref, A_HEAD_DIM ** -0.5 * LOG2E,
             head_dim=A_HEAD_DIM)
    rope_all(dot(wk_ref), c64, s64, ka_ref)

    hidx = jnp.dot(x_ref[...], widx_ref[...], precision=lax.Precision.HIGHEST,
                   preferred_element_type=F32)
    slot = lane // IDX_DIM

    def split3(v):
        hi = v.astype(BF16).astype(F32)
        mid = (v - hi).astype(BF16).astype(F32)
        lo = (v - hi - mid).astype(BF16).astype(F32)
        return hi, mid, lo

    def idx_rope(g):
        return _rope_group(hidx[:, g * LANES:(g + 1) * LANES], c32, s32, low)

    k_hi, k_mid, k_lo = split3(idx_rope(2))
    ik_ref[:, :LANES] = jnp.where(slot == 1, k_mid, jnp.where(slot == 3, k_lo, k_hi)).astype(BF16)
    ik_ref[:, LANES:] = jnp.where(slot == 0, k_hi, jnp.where(slot == 1, k_mid, 0.0)).astype(BF16)
    slots = LANES // IDX_DIM
    for g in range(IDX_HEADS // slots):
        rolled = [[p if d == 0 else pltpu.roll(p, d * IDX_DIM, 1) for d in range(slots)]
                  for p in split3(idx_rope(g))]
        for s_ in range(slots):
            at = lambda piece, t: rolled[piece][(t - s_) % slots]
            b1 = jnp.where(slot == 0, at(0, 0), jnp.where(slot == 1, at(0, 1),
                           jnp.where(slot == 2, at(1, 2), at(0, 3))))
            b2 = jnp.where(slot == 0, at(2, 0), jnp.where(slot == 1, at(1, 1), 0.0))
            o = (g * slots + s_) * 2 * LANES
            iq_ref[:, o:o + LANES] = b1.astype(BF16)
            iq_ref[:, o + LANES:o + 2 * LANES] = b2.astype(BF16)

    def rmsnorm(c, g_ref):
        ms = jnp.mean(c * c, axis=-1, keepdims=True)
        return (c * lax.rsqrt(ms + RMS_EPS) * g_ref[...]).astype(BF16)

    cqn = rmsnorm(dot(wcq_ref), gq_ref)
    qb = jnp.dot(cqn, wuq_ref[...], preferred_element_type=F32)
    pe_lane = (lane >= B_NOPE_DIM) & (lane < B_NOPE_DIM + B_ROPE_DIM)
    b_scale = (B_NOPE_DIM + B_ROPE_DIM) ** -0.5 * LOG2E
    for h in range(B_HEADS):
        g = qb[:, h * LANES:(h + 1) * LANES]
        r = jnp.where(pe_lane, _rope_group(g, c32, s32, low), g)
        qb_ref[:, h * LANES:(h + 1) * LANES] = (r * b_scale).astype(BF16)

    ckvn = rmsnorm(dot(wckv_ref), gkv_ref)
    kpe = _rope_group(dot(wkpe_ref), c32, s32, low)
    kn = jnp.dot(ckvn, wukvk_ref[...], preferred_element_type=F32)
    for h in range(B_HEADS):
        kb_ref[:, h * LANES:(h + 1) * LANES] = (kn[:, h * LANES:(h + 1) * LANES] + kpe).astype(BF16)

    def store_t(val, out_ref):
        for j in range(tm // CHUNK):
            out_ref[j] = val[:, j * CHUNK:(j + 1) * CHUNK].astype(out_ref.dtype)

    store_t(_nt(wukvvT_ref[...], ckvn), vbT_ref)
    store_t(_nt(wvT_ref[...], xb), vaT_ref)
    gt = _nt(wgT_ref[...], xb)
    store_t(gt * jax.nn.sigmoid(gt), gT_ref)
    iw_t = jnp.transpose(hidx[:, 3 * LANES:4 * LANES])
    store_t(iw_t[:16] * (IDX_DIM ** -0.5 * IDX_HEADS ** -0.5), iwT_ref)


def _projection(x2, pos_col, weights):
    n = x2.shape[0]
    tm = ROW_TILE
    nchunks = n // CHUNK
    cpt = tm // CHUNK

    def rows(width):
        return pl.BlockSpec((tm, width), lambda t: (t, 0))

    def full(a):
        return pl.BlockSpec(a.shape, lambda t: (0,) * a.ndim)

    def tspec(feat):
        return pl.BlockSpec((cpt, feat, CHUNK), lambda t: (t, 0, 0))

    out_shape = [
        jax.ShapeDtypeStruct((n, A_HEADS * LANES), BF16),
        jax.ShapeDtypeStruct((n, A_WIDTH), BF16),
        jax.ShapeDtypeStruct((n, IDX_HEADS * 2 * LANES), BF16),
        jax.ShapeDtypeStruct((n, 2 * LANES), BF16),
        jax.ShapeDtypeStruct((n, B_HEADS * LANES), BF16),
        jax.ShapeDtypeStruct((n, B_HEADS * LANES), BF16),
        jax.ShapeDtypeStruct((nchunks, A_WIDTH, CHUNK), BF16),
        jax.ShapeDtypeStruct((nchunks, B_WIDTH, CHUNK), BF16),
        jax.ShapeDtypeStruct((nchunks, A_WIDTH + B_WIDTH, CHUNK), F32),
        jax.ShapeDtypeStruct((nchunks, 16, CHUNK), F32),
    ]
    out_specs = [rows(A_HEADS * LANES), rows(A_WIDTH), rows(IDX_HEADS * 2 * LANES), rows(2 * LANES),
                 rows(B_HEADS * LANES), rows(B_HEADS * LANES),
                 tspec(A_WIDTH), tspec(B_WIDTH), tspec(A_WIDTH + B_WIDTH), tspec(16)]
    patterns = _rope_patterns()
    in_specs = [rows(D_MODEL), rows(1), full(patterns)] + [full(w) for w in weights]
    return pl.pallas_call(
        _proj_kernel,
        grid=(n // tm,),
        in_specs=in_specs,
        out_specs=out_specs,
        out_shape=out_shape,
        compiler_params=pltpu.CompilerParams(dimension_semantics=("parallel",),
                                             vmem_limit_bytes=VMEM_LIMIT),
        name="projection",
    )(x2, pos_col, patterns, *weights)


def _tiles(x):
    return [x[j * PACK:(j + 1) * PACK] for j in range(x.shape[0] // PACK)]


def _col_max(sb):
    top = _tiles(sb)
    while len(top) > 1:
        top = [jnp.maximum(top[j], top[j + 1]) for j in range(0, len(top), 2)]
    return jnp.max(top[0].astype(F32), axis=0, keepdims=True)


def _softmax_step(sb, smax, vt, m_ref, acc_ref, h):
    m_old = m_ref[h]
    m_new = jnp.maximum(m_old, smax)
    shift = jnp.broadcast_to(m_new, (PACK, m_new.shape[1])).astype(BF16)
    p = jnp.concatenate([jnp.exp2(t - shift) for t in _tiles(sb)], axis=0)
    vt_ones = jnp.concatenate([vt, jnp.ones((L_ROWS, vt.shape[1]), BF16)], axis=0)
    pv = jnp.dot(vt_ones, p, preferred_element_type=F32)
    acc_ref[h] = jnp.exp2(m_old - m_new) * acc_ref[h] + pv
    m_ref[h] = m_new


def _attention_pipeline(last, heads, logits, consume, s_ref, smax_ref):
    assert LOOKAHEAD < heads

    def issue(c, h):
        sb = logits(c, h)
        s_ref[h] = sb
        smax_ref[h] = _col_max(sb)

    def step(c, is_last):
        for h in range(heads):
            if h + LOOKAHEAD < heads:
                issue(c, h + LOOKAHEAD)
            elif not is_last:
                issue(c + 1, h + LOOKAHEAD - heads)
            consume(c, h, s_ref[h], smax_ref[h], is_last)

    def body(c, carry):
        step(c, False)
        return carry

    for h in range(LOOKAHEAD):
        issue(0, h)
    lax.fori_loop(0, last, body, 0)
    step(last, True)


def _softmax_init(m_ref, acc_ref):
    m_ref[...] = jnp.full(m_ref.shape, NEG, F32)
    acc_ref[...] = jnp.zeros(acc_ref.shape, F32)


def _softmax_finish(acc_ref, g_ref, o_ref, heads, dv):
    for h in range(heads):
        acc = acc_ref[h]
        o = acc[:dv] * (1.0 / acc[dv:dv + 1])
        o_ref[h * dv:(h + 1) * dv, :] = (o * g_ref[h * dv:(h + 1) * dv, :]).astype(o_ref.dtype)


def _dsa_kernel(iq_ref, iwT_ref, ik_ref, qa_ref, ka_ref, vaT_ref, g_ref, o_ref,
                hi_ref, lo_ref, bk_ref, bias_ref, m_ref, acc_ref, s_ref, smax_ref, *, k_top):
    C = CHUNK
    i = pl.program_id(1)
    nk = i + 1
    row = lax.broadcasted_iota(I32, (C, C), 0)
    col = lax.broadcasted_iota(I32, (C, C), 1)
    w = iwT_ref[...]

    def chunk_rows(c):
        return pl.ds(pl.multiple_of(c * C, C), C)

    def score_chunk(c, carry):
        ikc = ik_ref[chunk_rows(c), :]
        acc = jnp.zeros((C, C), F32)
        for h in range(IDX_HEADS):
            lg = _nt(ikc, iq_ref[:, h * 2 * LANES:(h + 1) * 2 * LANES])
            acc = acc + w[h:h + 1, :] * jnp.maximum(lg, 0.0)
        acc = acc + 0.0
        bits = lax.bitcast_convert_type(acc, I32)
        key = jnp.where(bits < 0, bits ^ jnp.int32(0x7FFFFFFF), bits)
        valid = (c * C + row) <= (i * C + col)
        key = jnp.where(valid, key, jnp.int32(INT_MIN))
        hi_ref[chunk_rows(c), :] = (key >> 16).astype(I16)
        lo_ref[chunk_rows(c), :] = (key ^ jnp.int32(0x8000)).astype(I16)
        return carry

    lax.fori_loop(0, nk, score_chunk, 0)

    @pl.when(nk % 2 == 1)
    def _():
        hi_ref[chunk_rows(nk), :] = jnp.full((C, C), MIN16, I16)
        lo_ref[chunk_rows(nk), :] = jnp.full((C, C), MIN16, I16)

    n_pairs = (nk + 1) // 2

    def pair_rows(d):
        return pl.ds(pl.multiple_of(d * 2 * C, 2 * C), 2 * C)

    def pack16(v):
        return jnp.broadcast_to(v, (PACK, C)).astype(I16)

    def tiles(x):
        return [x[j * PACK:(j + 1) * PACK] for j in range(x.shape[0] // PACK)]

    def tree_sum(parts):
        while len(parts) > 1:
            parts = [parts[j] + parts[j + 1] for j in range(0, len(parts), 2)]
        return parts[0]

    def count16(ref, pred):
        def body(d, acc):
            hits = [jnp.where(pred(t), jnp.int16(1), jnp.int16(0)) for t in tiles(ref[pair_rows(d), :])]
            return acc + tree_sum(hits)
        acc = lax.fori_loop(0, n_pairs, body, jnp.zeros((PACK, C), I16))
        return jnp.sum(acc.astype(I32), axis=0, keepdims=True)

    def kth_largest(ref, k):
        def bit_body(b, v):
            cand = v + jnp.left_shift(jnp.int32(1), 15 - b)
            c16 = pack16(cand)
            return jnp.where(count16(ref, lambda t: t >= c16) >= k, cand, v)
        return lax.fori_loop(0, 16, bit_body, jnp.full((1, C), MIN16, I32))

    thr_hi = kth_largest(hi_ref, k_top)
    hi16 = pack16(thr_hi)

    def bucket_pair(d, acc):
        his, los = tiles(hi_ref[pair_rows(d), :]), tiles(lo_ref[pair_rows(d), :])
        for j, (th, tl) in enumerate(zip(his, los)):
            bk_ref[pl.ds(pl.multiple_of(d * 2 * C, 2 * C) + j * PACK, PACK), :] = jnp.where(
                th == hi16, tl, jnp.int16(MIN16))
        return acc + tree_sum([jnp.where(th > hi16, jnp.int16(1), jnp.int16(0)) for th in his])

    above = lax.fori_loop(0, n_pairs, bucket_pair, jnp.zeros((PACK, C), I16))
    rank = k_top - jnp.sum(above.astype(I32), axis=0, keepdims=True)
    thr_lo = kth_largest(bk_ref, rank)
    lo16 = pack16(thr_lo)

    need = (rank - count16(bk_ref, lambda t: t > lo16)).astype(F32)
    tri = (row >= col).astype(BF16)

    def bias_chunk(c, carry):
        hi = hi_ref[chunk_rows(c), :].astype(I32)
        lo = lo_ref[chunk_rows(c), :].astype(I32)
        same_hi = hi == thr_hi
        eq = same_hi & (lo == thr_lo)
        gt = (hi > thr_hi) | (same_hi & (lo > thr_lo))
        prefix = jnp.dot(tri, eq.astype(BF16), preferred_element_type=F32) + carry
        sel = gt | (eq & (prefix <= need))
        valid = (c * C + row) <= (i * C + col)
        bias_ref[chunk_rows(c), :] = jnp.where(sel & valid, 0.0, NEG).astype(BF16)
        return carry + jnp.sum(eq.astype(F32), axis=0, keepdims=True)

    lax.fori_loop(0, nk, bias_chunk, jnp.zeros((1, C), F32))

    _softmax_init(m_ref, acc_ref)

    def logits(c, h):
        s = _nt(ka_ref[chunk_rows(c), (h // 2) * LANES:(h // 2 + 1) * LANES],
                qa_ref[:, h * LANES:(h + 1) * LANES])
        return s.astype(BF16) + bias_ref[chunk_rows(c), :]

    def consume(c, h, s, smax, is_last):
        _softmax_step(s, smax, vaT_ref[c, h * A_HEAD_DIM:(h + 1) * A_HEAD_DIM, :], m_ref, acc_ref, h)

    _attention_pipeline(i, A_HEADS, logits, consume, s_ref, smax_ref)
    _softmax_finish(acc_ref, g_ref, o_ref, A_HEADS, A_HEAD_DIM)


def _dsa_attention(iq, iwT, ik, qa, ka, vaT, gT, batch, seq):
    C = CHUNK
    nc = seq // C
    k_top = min(TOPK_MAX, seq // 4)
    sel_rows = (nc + nc % 2) * C
    return pl.pallas_call(
        functools.partial(_dsa_kernel, k_top=k_top),
        grid=(batch, nc),
        in_specs=[
            pl.BlockSpec((C, IDX_HEADS * 2 * LANES), lambda b, i: (b * nc + i, 0)),
            pl.BlockSpec((None, 16, C), lambda b, i: (b * nc + i, 0, 0)),
            pl.BlockSpec((seq, 2 * LANES), lambda b, i: (b, 0)),
            pl.BlockSpec((C, A_HEADS * LANES), lambda b, i: (b * nc + i, 0)),
            pl.BlockSpec((seq, A_WIDTH), lambda b, i: (b, 0)),
            pl.BlockSpec((nc, A_WIDTH, C), lambda b, i: (b, 0, 0)),
            pl.BlockSpec((None, A_WIDTH, C), lambda b, i: (b * nc + i, 0, 0)),
        ],
        out_specs=pl.BlockSpec((None, A_WIDTH, C), lambda b, i: (b * nc + i, 0, 0)),
        out_shape=jax.ShapeDtypeStruct((batch * nc, A_WIDTH, C), BF16),
        scratch_shapes=[
            pltpu.VMEM((sel_rows, C), I16),
            pltpu.VMEM((sel_rows, C), I16),
            pltpu.VMEM((sel_rows, C), I16),
            pltpu.VMEM((seq, C), BF16),
            pltpu.VMEM((A_HEADS, 1, C), F32),
            pltpu.VMEM((A_HEADS, A_HEAD_DIM + L_ROWS, C), F32),
            pltpu.VMEM((A_HEADS, C, C), BF16),
            pltpu.VMEM((A_HEADS, 1, C), F32),
        ],
        compiler_params=pltpu.CompilerParams(dimension_semantics=("parallel", "arbitrary"),
                                             vmem_limit_bytes=VMEM_LIMIT),
        name="dsa_attention",
    )(iq, iwT, ik, qa, ka, vaT, gT)


def _mla_kernel(qb_ref, kb_ref, vbT_ref, g_ref, o_ref, m_ref, acc_ref, s_ref, smax_ref):
    C = CHUNK
    i = pl.program_id(1)
    row = lax.broadcasted_iota(I32, (C, C), 0)
    col = lax.broadcasted_iota(I32, (C, C), 1)
    causal = row <= col
    _softmax_init(m_ref, acc_ref)

    def logits(c, h):
        kc = kb_ref[pl.ds(pl.multiple_of(c * C, C), C), h * LANES:(h + 1) * LANES]
        return _nt(kc, qb_ref[:, h * LANES:(h + 1) * LANES]).astype(BF16)

    def consume(c, h, s, smax, is_last):
        if is_last:
            s = jnp.where(causal, s, jnp.asarray(NEG, BF16))
            smax = _col_max(s)
        _softmax_step(s, smax, vbT_ref[c, h * B_V_DIM:(h + 1) * B_V_DIM, :], m_ref, acc_ref, h)

    _attention_pipeline(i, B_HEADS, logits, consume, s_ref, smax_ref)
    _softmax_finish(acc_ref, g_ref, o_ref, B_HEADS, B_V_DIM)


def _mla_attention(qb, kb, vbT, gT, batch, seq):
    C = CHUNK
    nc = seq // C
    return pl.pallas_call(
        _mla_kernel,
        grid=(batch, nc),
        in_specs=[
            pl.BlockSpec((C, B_HEADS * LANES), lambda b, i: (b * nc + i, 0)),
            pl.BlockSpec((seq, B_HEADS * LANES), lambda b, i: (b, 0)),
            pl.BlockSpec((nc, B_WIDTH, C), lambda b, i: (b, 0, 0)),
            pl.BlockSpec((None, B_WIDTH, C), lambda b, i: (b * nc + i, 1, 0)),
        ],
        out_specs=pl.BlockSpec((None, B_WIDTH, C), lambda b, i: (b * nc + i, 0, 0)),
        out_shape=jax.ShapeDtypeStruct((batch * nc, B_WIDTH, C), BF16),
        scratch_shapes=[
            pltpu.VMEM((B_HEADS, 1, C), F32),
            pltpu.VMEM((B_HEADS, B_V_DIM + L_ROWS, C), F32),
            pltpu.VMEM((B_HEADS, C, C), BF16),
            pltpu.VMEM((B_HEADS, 1, C), F32),
        ],
        compiler_params=pltpu.CompilerParams(dimension_semantics=("parallel", "arbitrary"),
                                             vmem_limit_bytes=VMEM_LIMIT),
        name="mla_attention",
    )(qb, kb, vbT, gT)


def _out_kernel(oa_ref, ob_ref, x_ref, w_ref, lng_ref, lnb_ref, y_ref, *, alpha):
    for j in range(oa_ref.shape[0]):
        rows = slice(j * CHUNK, (j + 1) * CHUNK)
        out = (lax.dot_general(oa_ref[j], w_ref[:A_WIDTH, :], TN_DIMS, preferred_element_type=F32)
               + lax.dot_general(ob_ref[j], w_ref[A_WIDTH:, :], TN_DIMS, preferred_element_type=F32))
        z = alpha * x_ref[rows, :] + out
        mu = jnp.mean(z, axis=-1, keepdims=True)
        zc = z - mu
        var = jnp.mean(zc * zc, axis=-1, keepdims=True)
        y_ref[rows, :] = zc * lax.rsqrt(var + LN_EPS) * lng_ref[...] + lnb_ref[...]


def _output(oaT, obT, x2, w_out, ln_g, ln_b, alpha):
    C = CHUNK
    n = x2.shape[0]
    cpt = ROW_TILE // C
    const = lambda a: pl.BlockSpec(a.shape, lambda t: (0,) * a.ndim)
    return pl.pallas_call(
        functools.partial(_out_kernel, alpha=alpha),
        grid=(n // ROW_TILE,),
        in_specs=[
            pl.BlockSpec((cpt, A_WIDTH, C), lambda t: (t, 0, 0)),
            pl.BlockSpec((cpt, B_WIDTH, C), lambda t: (t, 0, 0)),
            pl.BlockSpec((ROW_TILE, D_MODEL), lambda t: (t, 0)),
            const(w_out), const(ln_g), const(ln_b),
        ],
        out_specs=pl.BlockSpec((ROW_TILE, D_MODEL), lambda t: (t, 0)),
        out_shape=jax.ShapeDtypeStruct((n, D_MODEL), F32),
        compiler_params=pltpu.CompilerParams(dimension_semantics=("parallel",),
                                             vmem_limit_bytes=VMEM_LIMIT),
        name="output",
    )(oaT, obT, x2, w_out, ln_g, ln_b)


def _prepare_weights(w_in, q_norm_g, w_uq, kv_norm_g, w_ukv):
    offs = np.concatenate([[0], np.cumsum(IN_SPLITS)])
    (wqa, wka, wva, wga, wiq, wik, wiw, wcq, wckv, wkr, wgb) = [
        w_in[:, int(offs[j]):int(offs[j + 1])] for j in range(len(IN_SPLITS))]
    d = w_in.shape[0]
    zeros = lambda c: jnp.zeros((d, c), w_in.dtype)
    widx = jnp.concatenate([wiq] + [wik] * (LANES // IDX_DIM) + [wiw, zeros(LANES - IDX_HEADS)], axis=1)
    wkpe = jnp.concatenate([zeros(B_NOPE_DIM), wkr, zeros(LANES - B_NOPE_DIM - B_ROPE_DIM)], axis=1)
    wgT = jnp.concatenate([wga, wgb], axis=1).T

    qk = B_NOPE_DIM + B_ROPE_DIM
    wuq = w_uq.reshape(Q_LORA, B_HEADS, qk)
    wuq = jnp.pad(wuq, ((0, 0), (0, 0), (0, LANES - qk))).reshape(Q_LORA, B_HEADS * LANES)
    wukv = w_ukv.reshape(KV_LORA, B_HEADS, B_NOPE_DIM + B_V_DIM)
    wukvk = jnp.pad(wukv[:, :, :B_NOPE_DIM], ((0, 0), (0, 0), (0, LANES - B_NOPE_DIM)))
    wukvk = wukvk.reshape(KV_LORA, B_HEADS * LANES)
    wukvvT = wukv[:, :, B_NOPE_DIM:].reshape(KV_LORA, B_WIDTH).T

    bf = lambda a: a.astype(BF16)
    order = (np.arange(A_HEADS)[:, None] * A_HEAD_DIM + _head64_order()[None, :]).reshape(-1)
    return [bf(wqa[:, order]), bf(wka[:, order]), widx.astype(F32), bf(wkpe), bf(wcq), bf(wckv),
            bf(wva.T), bf(wgT),
            q_norm_g.reshape(1, Q_LORA).astype(F32), kv_norm_g.reshape(1, KV_LORA).astype(F32),
            bf(wuq), bf(wukvk), bf(wukvvT)]


def kernel(x, positions, w_in, q_norm_g, w_uq, kv_norm_g, w_ukv, w_out, ln_g, ln_b):
    batch, seq, d_model = x.shape
    depth = w_in.shape[0]
    alpha = (2 * depth) ** 0.25
    assert d_model == D_MODEL and seq % ROW_TILE == 0 and ROW_TILE % CHUNK == 0
    n = batch * seq
    pos_col = positions.reshape(n, 1).astype(I32)
    x2 = x.reshape(n, d_model)
    for l in range(depth):
        weights = _prepare_weights(w_in[l], q_norm_g[l], w_uq[l], kv_norm_g[l], w_ukv[l])
        qa, ka, iq, ik, qb, kb, vaT, vbT, gT, iwT = _projection(x2, pos_col, weights)
        oaT = _dsa_attention(iq, iwT, ik, qa, ka, vaT, gT, batch, seq)
        obT = _mla_attention(qb, kb, vbT, gT, batch, seq)
        x2 = _output(oaT, obT, x2, w_out[l].astype(BF16),
                     ln_g[l].reshape(1, d_model), ln_b[l].reshape(1, d_model), alpha)
    return x2.reshape(batch, seq, d_model)
```

```python
import functools

import jax
import jax.numpy as jnp
import numpy as np
from jax import lax
from jax.experimental import pallas as pl
from jax.experimental.pallas import tpu as pltpu

F32 = jnp.float32
BF16 = jnp.bfloat16
I32 = jnp.int32
I16 = jnp.int16

D_MODEL = 1024
A_HEADS = 8
A_HEAD_DIM = 64
A_WIDTH = A_HEADS * A_HEAD_DIM
IDX_HEADS = 8
IDX_DIM = 32
TOPK_MAX = 256
B_HEADS = 8
B_NOPE_DIM = 64
B_ROPE_DIM = 32
B_V_DIM = 64
B_WIDTH = B_HEADS * B_V_DIM
Q_LORA = 256
KV_LORA = 128
ROPE_THETA = 10000.0
LN_EPS = 1e-5
RMS_EPS = 1e-6

IN_SPLITS = (A_WIDTH, A_WIDTH, A_WIDTH, A_WIDTH, IDX_HEADS * IDX_DIM, IDX_DIM, IDX_HEADS,
             Q_LORA, KV_LORA, B_ROPE_DIM, B_WIDTH)

LANES = 128
CHUNK = 256
ROW_TILE = 512
VMEM_LIMIT = 56 * 1024 * 1024
NEG = -1e30
INT_MIN = -2 ** 31
MIN16 = -2 ** 15
PACK = 16
LOG2E = 1.4426950408889634
L_ROWS = 16
LOOKAHEAD = 6

NT_DIMS = (((1,), (1,)), ((), ()))
TN_DIMS = (((0,), (0,)), ((), ()))


def _nt(a, b):
    return lax.dot_general(a, b, NT_DIMS, preferred_element_type=F32)


FREQ_BITS = 12


ROPE_HALF = IDX_DIM // 2


def _head64_order():
    i = np.arange(ROPE_HALF)
    half = A_HEAD_DIM // 2
    return np.concatenate([2 * i, half + 2 * i, 2 * i + 1, half + 2 * i + 1])


def _rope_patterns():
    lane = np.arange(LANES)
    freq_index = 2 * (lane % ROPE_HALF) + (lane // IDX_DIM) % 2
    rest = ROPE_THETA ** (-2.0 * freq_index.astype(np.float64) / A_HEAD_DIM)
    rows = []
    for _ in range(2):
        mant, exp = np.frexp(rest)
        piece = np.ldexp(np.floor(mant * 2 ** FREQ_BITS) / 2 ** FREQ_BITS, exp)
        rows.append(piece)
        rest = rest - piece
    rows.append(rest)
    rows.append(np.where(lane % IDX_DIM < ROPE_HALF, -1.0, 1.0))
    return jnp.asarray(np.stack(rows), F32)


def _rope_tables(pos_ref, pat_ref, lane):
    pos = pos_ref[...].astype(F32)
    a0 = pos * pat_ref[0:1, :]
    dl = pos * pat_ref[1:2, :] + pos * pat_ref[2:3, :]
    c0, s0, cd, sd = jnp.cos(a0), jnp.sin(a0), jnp.cos(dl), jnp.sin(dl)
    c64 = c0 * cd - s0 * sd
    s64 = (s0 * cd + c0 * sd) * pat_ref[3:4, :]
    odd_block = (lane // IDX_DIM) % 2 == 1
    c32 = jnp.where(odd_block, pltpu.roll(c64, IDX_DIM, 1), c64)
    s32 = jnp.where(odd_block, pltpu.roll(s64, IDX_DIM, 1), s64)
    return c64, s64, c32, s32


def _rope_group(xg, cos, sin_signed, low):
    swapped = jnp.where(low, pltpu.roll(xg, LANES - ROPE_HALF, 1), pltpu.roll(xg, ROPE_HALF, 1))
    return xg * cos + swapped * sin_signed


def _proj_kernel(x_ref, pos_ref, pat_ref,
                 wq_ref, wk_ref, widx_ref, wkpe_ref, wcq_ref, wckv_ref,
                 wvT_ref, wgT_ref, gq_ref, gkv_ref, wuq_ref, wukvk_ref, wukvvT_ref,
                 qa_ref, ka_ref, iq_ref, ik_ref, qb_ref, kb_ref,
                 vaT_ref, vbT_ref, gT_ref, iwT_ref):
    tm = x_ref.shape[0]
    xb = x_ref[...].astype(BF16)
    lane = lax.broadcasted_iota(I32, (tm, LANES), 1)
    c64, s64, c32, s32 = _rope_tables(pos_ref, pat_ref, lane)
    low = (lane % IDX_DIM) < ROPE_HALF

    def dot(w_ref):
        return jnp.dot(xb, w_ref[...], preferred_element_type=F32)

    def rope_all(h, cos, sin, out_ref, scale=None, head_dim=None):
        for g in range(h.shape[1] // LANES):
            r = _rope_group(h[:, g * LANES:(g + 1) * LANES], cos, sin, low)
            if scale is not None:
                r = r * scale
            if head_dim is None:
                out_ref[:, g * LANES:(g + 1) * LANES] = r.astype(out_ref.dtype)
                continue
            per_group = LANES // head_dim
            for j in range(per_group):
                o = (g * per_group + j) * LANES
                out_ref[:, o:o + LANES] = jnp.where(lane // head_dim == j, r, 0.0).astype(out_ref.dtype)

    rope_all(dot(wq_ref), c64, s64, qa_ref, A_HEAD_DIM ** -0.5 * LOG2E,
             head_dim=A_HEAD_DIM)
    rope_all(dot(wk_ref), c64, s64, ka_ref)

    hidx = jnp.dot(x_ref[...], widx_ref[...], precision=lax.Precision.HIGHEST,
                   preferred_element_type=F32)
    slot = lane // IDX_DIM

    def split3(v):
        hi = v.astype(BF16).astype(F32)
        mid = (v - hi).astype(BF16).astype(F32)
        lo = (v - hi - mid).astype(BF16).astype(F32)
        return hi, mid, lo

    def idx_rope(g):
        return _rope_group(hidx[:, g * LANES:(g + 1) * LANES], c32, s32, low)

    k_hi, k_mid, k_lo = split3(idx_rope(2))
    ik_ref[:, :LANES] = jnp.where(slot == 1, k_mid, jnp.where(slot == 3, k_lo, k_hi)).astype(BF16)
    ik_ref[:, LANES:] = jnp.where(slot == 0, k_hi, jnp.where(slot == 1, k_mid, 0.0)).astype(BF16)
    slots = LANES // IDX_DIM
    for g in range(IDX_HEADS // slots):
        rolled = [[p if d == 0 else pltpu.roll(p, d * IDX_DIM, 1) for d in range(slots)]
                  for p in split3(idx_rope(g))]
        for s_ in range(slots):
            at = lambda piece, t: rolled[piece][(t - s_) % slots]
            b1 = jnp.where(slot == 0, at(0, 0), jnp.where(slot == 1, at(0, 1),
                           jnp.where(slot == 2, at(1, 2), at(0, 3))))
            b2 = jnp.where(slot == 0, at(2, 0), jnp.where(slot == 1, at(1, 1), 0.0))
            o = (g * slots + s_) * 2 * LANES
            iq_ref[:, o:o + LANES] = b1.astype(BF16)
            iq_ref[:, o + LANES:o + 2 * LANES] = b2.astype(BF16)

    def rmsnorm(c, g_ref):
        ms = jnp.mean(c * c, axis=-1, keepdims=True)
        return (c * lax.rsqrt(ms + RMS_EPS) * g_ref[...]).astype(BF16)

    cqn = rmsnorm(dot(wcq_ref), gq_ref)
    qb = jnp.dot(cqn, wuq_ref[...], preferred_element_type=F32)
    pe_lane = (lane >= B_NOPE_DIM) & (lane < B_NOPE_DIM + B_ROPE_DIM)
    b_scale = (B_NOPE_DIM + B_ROPE_DIM) ** -0.5 * LOG2E
    for h in range(B_HEADS):
        g = qb[:, h * LANES:(h + 1) * LANES]
        r = jnp.where(pe_lane, _rope_group(g, c32, s32, low), g)
        qb_ref[:, h * LANES:(h + 1) * LANES] = (r * b_scale).astype(BF16)

    ckvn = rmsnorm(dot(wckv_ref), gkv_ref)
    kpe = _rope_group(dot(wkpe_ref), c32, s32, low)
    kn = jnp.dot(ckvn, wukvk_ref[...], preferred_element_type=F32)
    for h in range(B_HEADS):
        kb_ref[:, h * LANES:(h + 1) * LANES] = (kn[:, h * LANES:(h + 1) * LANES] + kpe).astype(BF16)

    def store_t(val, out_ref):
        for j in range(tm // CHUNK):
            out_ref[j] = val[:, j * CHUNK:(j + 1) * CHUNK].astype(out_ref.dtype)

    store_t(_nt(wukvvT_ref[...], ckvn), vbT_ref)
    store_t(_nt(wvT_ref[...], xb), vaT_ref)
    gt = _nt(wgT_ref[...], xb)
    store_t(gt * jax.nn.sigmoid(gt), gT_ref)
    iw_t = jnp.transpose(hidx[:, 3 * LANES:4 * LANES])
    store_t(iw_t[:16] * (IDX_DIM ** -0.5 * IDX_HEADS ** -0.5), iwT_ref)


def _projection(x2, pos_col, weights):
    n = x2.shape[0]
    tm = ROW_TILE
    nchunks = n // CHUNK
    cpt = tm // CHUNK

    def rows(width):
        return pl.BlockSpec((tm, width), lambda t: (t, 0))

    def full(a):
        return pl.BlockSpec(a.shape, lambda t: (0,) * a.ndim)

    def tspec(feat):
        return pl.BlockSpec((cpt, feat, CHUNK), lambda t: (t, 0, 0))

    out_shape = [
        jax.ShapeDtypeStruct((n, A_HEADS * LANES), BF16),
        jax.ShapeDtypeStruct((n, A_WIDTH), BF16),
        jax.ShapeDtypeStruct((n, IDX_HEADS * 2 * LANES), BF16),
        jax.ShapeDtypeStruct((n, 2 * LANES), BF16),
        jax.ShapeDtypeStruct((n, B_HEADS * LANES), BF16),
        jax.ShapeDtypeStruct((n, B_HEADS * LANES), BF16),
        jax.ShapeDtypeStruct((nchunks, A_WIDTH, CHUNK), BF16),
        jax.ShapeDtypeStruct((nchunks, B_WIDTH, CHUNK), BF16),
        jax.ShapeDtypeStruct((nchunks, A_WIDTH + B_WIDTH, CHUNK), F32),
        jax.ShapeDtypeStruct((nchunks, 16, CHUNK), F32),
    ]
    out_specs = [rows(A_HEADS * LANES), rows(A_WIDTH), rows(IDX_HEADS * 2 * LANES), rows(2 * LANES),
                 rows(B_HEADS * LANES), rows(B_HEADS * LANES),
                 tspec(A_WIDTH), tspec(B_WIDTH), tspec(A_WIDTH + B_WIDTH), tspec(16)]
    patterns = _rope_patterns()
    in_specs = [rows(D_MODEL), rows(1), full(patterns)] + [full(w) for w in weights]
    return pl.pallas_call(
        _proj_kernel,
        grid=(n // tm,),
        in_specs=in_specs,
        out_specs=out_specs,
        out_shape=out_shape,
        compiler_params=pltpu.CompilerParams(dimension_semantics=("parallel",),
                                             vmem_limit_bytes=VMEM_LIMIT),
        name="projection",
    )(x2, pos_col, patterns, *weights)


def _softmax_step(s, smax, vt, m_ref, acc_ref, h):
    m_old = m_ref[h]
    m_new = jnp.maximum(m_old, smax)
    p = jnp.exp2(s - m_new).astype(BF16)
    vt_ones = jnp.concatenate([vt, jnp.ones((L_ROWS, vt.shape[1]), BF16)], axis=0)
    pv = jnp.dot(vt_ones, p, preferred_element_type=F32)
    acc_ref[h] = jnp.exp2(m_old - m_new) * acc_ref[h] + pv
    m_ref[h] = m_new


def _attention_pipeline(last, heads, logits, consume, s_ref, smax_ref):
    assert LOOKAHEAD < heads

    def issue(c, h):
        s = logits(c, h)
        s_ref[h] = s
        smax_ref[h] = jnp.max(s, axis=0, keepdims=True)

    def step(c, is_last):
        for h in range(heads):
            if h + LOOKAHEAD < heads:
                issue(c, h + LOOKAHEAD)
            elif not is_last:
                issue(c + 1, h + LOOKAHEAD - heads)
            consume(c, h, s_ref[h], smax_ref[h], is_last)

    for h in range(LOOKAHEAD):
        issue(0, h)

    odd = last % 2

    @pl.when(odd == 1)
    def _():
        step(0, False)

    def body(d, carry):
        c = odd + 2 * d
        step(c, False)
        step(c + 1, False)
        return carry

    lax.fori_loop(0, last // 2, body, 0)
    step(last, True)


def _softmax_init(m_ref, acc_ref):
    m_ref[...] = jnp.full(m_ref.shape, NEG, F32)
    acc_ref[...] = jnp.zeros(acc_ref.shape, F32)


def _softmax_finish(acc_ref, g_ref, o_ref, heads, dv):
    for h in range(heads):
        acc = acc_ref[h]
        o = acc[:dv] * (1.0 / acc[dv:dv + 1])
        o_ref[h * dv:(h + 1) * dv, :] = (o * g_ref[h * dv:(h + 1) * dv, :]).astype(o_ref.dtype)


def _dsa_kernel(iq_ref, iwT_ref, ik_ref, qa_ref, ka_ref, vaT_ref, g_ref, o_ref,
                hi_ref, lo_ref, bk_ref, bias_ref, m_ref, acc_ref, s_ref, smax_ref, *, k_top):
    C = CHUNK
    i = pl.program_id(1)
    nk = i + 1
    row = lax.broadcasted_iota(I32, (C, C), 0)
    col = lax.broadcasted_iota(I32, (C, C), 1)
    w = iwT_ref[...]

    def chunk_rows(c):
        return pl.ds(pl.multiple_of(c * C, C), C)

    def score_chunk(c, carry):
        ikc = ik_ref[chunk_rows(c), :]
        acc = jnp.zeros((C, C), F32)
        for h in range(IDX_HEADS):
            lg = _nt(ikc, iq_ref[:, h * 2 * LANES:(h + 1) * 2 * LANES])
            acc = acc + w[h:h + 1, :] * jnp.maximum(lg, 0.0)
        acc = acc + 0.0
        bits = lax.bitcast_convert_type(acc, I32)
        key = jnp.where(bits < 0, bits ^ jnp.int32(0x7FFFFFFF), bits)
        valid = (c * C + row) <= (i * C + col)
        key = jnp.where(valid, key, jnp.int32(INT_MIN))
        hi_ref[chunk_rows(c), :] = (key >> 16).astype(I16)
        lo_ref[chunk_rows(c), :] = (key ^ jnp.int32(0x8000)).astype(I16)
        return carry

    lax.fori_loop(0, nk, score_chunk, 0)

    @pl.when(nk % 2 == 1)
    def _():
        hi_ref[chunk_rows(nk), :] = jnp.full((C, C), MIN16, I16)
        lo_ref[chunk_rows(nk), :] = jnp.full((C, C), MIN16, I16)

    n_pairs = (nk + 1) // 2

    def pair_rows(d):
        return pl.ds(pl.multiple_of(d * 2 * C, 2 * C), 2 * C)

    def pack16(v):
        return jnp.broadcast_to(v, (PACK, C)).astype(I16)

    def tiles(x):
        return [x[j * PACK:(j + 1) * PACK] for j in range(x.shape[0] // PACK)]

    def tree_sum(parts):
        while len(parts) > 1:
            parts = [parts[j] + parts[j + 1] for j in range(0, len(parts), 2)]
        return parts[0]

    def count16(ref, pred):
        def body(d, acc):
            hits = [jnp.where(pred(t), jnp.int16(1), jnp.int16(0)) for t in tiles(ref[pair_rows(d), :])]
            return acc + tree_sum(hits)
        acc = lax.fori_loop(0, n_pairs, body, jnp.zeros((PACK, C), I16))
        return jnp.sum(acc.astype(I32), axis=0, keepdims=True)

    def kth_largest(ref, k):
        def bit_body(b, v):
            cand = v + jnp.left_shift(jnp.int32(1), 15 - b)
            c16 = pack16(cand)
            return jnp.where(count16(ref, lambda t: t >= c16) >= k, cand, v)
        return lax.fori_loop(0, 16, bit_body, jnp.full((1, C), MIN16, I32))

    thr_hi = kth_largest(hi_ref, k_top)
    hi16 = pack16(thr_hi)

    def bucket_pair(d, acc):
        his, los = tiles(hi_ref[pair_rows(d), :]), tiles(lo_ref[pair_rows(d), :])
        for j, (th, tl) in enumerate(zip(his, los)):
            bk_ref[pl.ds(pl.multiple_of(d * 2 * C, 2 * C) + j * PACK, PACK), :] = jnp.where(
                th == hi16, tl, jnp.int16(MIN16))
        return acc + tree_sum([jnp.where(th > hi16, jnp.int16(1), jnp.int16(0)) for th in his])

    above = lax.fori_loop(0, n_pairs, bucket_pair, jnp.zeros((PACK, C), I16))
    rank = k_top - jnp.sum(above.astype(I32), axis=0, keepdims=True)
    thr_lo = kth_largest(bk_ref, rank)
    lo16 = pack16(thr_lo)

    need = (rank - count16(bk_ref, lambda t: t > lo16)).astype(F32)
    tri = (row >= col).astype(BF16)

    def bias_chunk(c, carry):
        hi = hi_ref[chunk_rows(c), :].astype(I32)
        lo = lo_ref[chunk_rows(c), :].astype(I32)
        same_hi = hi == thr_hi
        eq = same_hi & (lo == thr_lo)
        gt = (hi > thr_hi) | (same_hi & (lo > thr_lo))
        prefix = jnp.dot(tri, eq.astype(BF16), preferred_element_type=F32) + carry
        sel = gt | (eq & (prefix <= need))
        valid = (c * C + row) <= (i * C + col)
        bias_ref[chunk_rows(c), :] = jnp.where(sel & valid, 0.0, NEG)
        return carry + jnp.sum(eq.astype(F32), axis=0, keepdims=True)

    lax.fori_loop(0, nk, bias_chunk, jnp.zeros((1, C), F32))

    _softmax_init(m_ref, acc_ref)

    def logits(c, h):
        s = _nt(ka_ref[chunk_rows(c), (h // 2) * LANES:(h // 2 + 1) * LANES],
                qa_ref[:, h * LANES:(h + 1) * LANES])
        return s + bias_ref[chunk_rows(c), :]

    def consume(c, h, s, smax, is_last):
        _softmax_step(s, smax, vaT_ref[c, h * A_HEAD_DIM:(h + 1) * A_HEAD_DIM, :], m_ref, acc_ref, h)

    _attention_pipeline(i, A_HEADS, logits, consume, s_ref, smax_ref)
    _softmax_finish(acc_ref, g_ref, o_ref, A_HEADS, A_HEAD_DIM)


def _dsa_attention(iq, iwT, ik, qa, ka, vaT, gT, batch, seq):
    C = CHUNK
    nc = seq // C
    k_top = min(TOPK_MAX, seq // 4)
    sel_rows = (nc + nc % 2) * C
    return pl.pallas_call(
        functools.partial(_dsa_kernel, k_top=k_top),
        grid=(batch, nc),
        in_specs=[
            pl.BlockSpec((C, IDX_HEADS * 2 * LANES), lambda b, i: (b * nc + i, 0)),
            pl.BlockSpec((None, 16, C), lambda b, i: (b * nc + i, 0, 0)),
            pl.BlockSpec((seq, 2 * LANES), lambda b, i: (b, 0)),
            pl.BlockSpec((C, A_HEADS * LANES), lambda b, i: (b * nc + i, 0)),
            pl.BlockSpec((seq, A_WIDTH), lambda b, i: (b, 0)),
            pl.BlockSpec((nc, A_WIDTH, C), lambda b, i: (b, 0, 0)),
            pl.BlockSpec((None, A_WIDTH, C), lambda b, i: (b * nc + i, 0, 0)),
        ],
        out_specs=pl.BlockSpec((None, A_WIDTH, C), lambda b, i: (b * nc + i, 0, 0)),
        out_shape=jax.ShapeDtypeStruct((batch * nc, A_WIDTH, C), BF16),
        scratch_shapes=[
            pltpu.VMEM((sel_rows, C), I16),
            pltpu.VMEM((sel_rows, C), I16),
            pltpu.VMEM((sel_rows, C), I16),
            pltpu.VMEM((seq, C), F32),
            pltpu.VMEM((A_HEADS, 1, C), F32),
            pltpu.VMEM((A_HEADS, A_HEAD_DIM + L_ROWS, C), F32),
            pltpu.VMEM((A_HEADS, C, C), F32),
            pltpu.VMEM((A_HEADS, 1, C), F32),
        ],
        compiler_params=pltpu.CompilerParams(dimension_semantics=("parallel", "arbitrary"),
                                             vmem_limit_bytes=VMEM_LIMIT),
        name="dsa_attention",
    )(iq, iwT, ik, qa, ka, vaT, gT)


def _mla_kernel(qb_ref, kb_ref, vbT_ref, g_ref, o_ref, m_ref, acc_ref, s_ref, smax_ref):
    C = CHUNK
    i = pl.program_id(1)
    row = lax.broadcasted_iota(I32, (C, C), 0)
    col = lax.broadcasted_iota(I32, (C, C), 1)
    causal = row <= col
    _softmax_init(m_ref, acc_ref)

    def logits(c, h):
        kc = kb_ref[pl.ds(pl.multiple_of(c * C, C), C), h * LANES:(h + 1) * LANES]
        return _nt(kc, qb_ref[:, h * LANES:(h + 1) * LANES])

    def consume(c, h, s, smax, is_last):
        if is_last:
            s = jnp.where(causal, s, NEG)
            smax = jnp.max(s, axis=0, keepdims=True)
        _softmax_step(s, smax, vbT_ref[c, h * B_V_DIM:(h + 1) * B_V_DIM, :], m_ref, acc_ref, h)

    _attention_pipeline(i, B_HEADS, logits, consume, s_ref, smax_ref)
    _softmax_finish(acc_ref, g_ref, o_ref, B_HEADS, B_V_DIM)


def _mla_attention(qb, kb, vbT, gT, batch, seq):
    C = CHUNK
    nc = seq // C
    return pl.pallas_call(
        _mla_kernel,
        grid=(batch, nc),
        in_specs=[
            pl.BlockSpec((C, B_HEADS * LANES), lambda b, i: (b * nc + i, 0)),
            pl.BlockSpec((seq, B_HEADS * LANES), lambda b, i: (b, 0)),
            pl.BlockSpec((nc, B_WIDTH, C), lambda b, i: (b, 0, 0)),
            pl.BlockSpec((None, B_WIDTH, C), lambda b, i: (b * nc + i, 1, 0)),
        ],
        out_specs=pl.BlockSpec((None, B_WIDTH, C), lambda b, i: (b * nc + i, 0, 0)),
        out_shape=jax.ShapeDtypeStruct((batch * nc, B_WIDTH, C), BF16),
        scratch_shapes=[
            pltpu.VMEM((B_HEADS, 1, C), F32),
            pltpu.VMEM((B_HEADS, B_V_DIM + L_ROWS, C), F32),
            pltpu.VMEM((B_HEADS, C, C), F32),
            pltpu.VMEM((B_HEADS, 1, C), F32),
        ],
        compiler_params=pltpu.CompilerParams(dimension_semantics=("parallel", "arbitrary"),
                                             vmem_limit_bytes=VMEM_LIMIT),
        name="mla_attention",
    )(qb, kb, vbT, gT)


def _out_kernel(oa_ref, ob_ref, x_ref, w_ref, lng_ref, lnb_ref, y_ref, *, alpha):
    for j in range(oa_ref.shape[0]):
        rows = slice(j * CHUNK, (j + 1) * CHUNK)
        out = (lax.dot_general(oa_ref[j], w_ref[:A_WIDTH, :], TN_DIMS, preferred_element_type=F32)
               + lax.dot_general(ob_ref[j], w_ref[A_WIDTH:, :], TN_DIMS, preferred_element_type=F32))
        z = alpha * x_ref[rows, :] + out
        mu = jnp.mean(z, axis=-1, keepdims=True)
        zc = z - mu
        var = jnp.mean(zc * zc, axis=-1, keepdims=True)
        y_ref[rows, :] = zc * lax.rsqrt(var + LN_EPS) * lng_ref[...] + lnb_ref[...]


def _output(oaT, obT, x2, w_out, ln_g, ln_b, alpha):
    C = CHUNK
    n = x2.shape[0]
    cpt = ROW_TILE // C
    const = lambda a: pl.BlockSpec(a.shape, lambda t: (0,) * a.ndim)
    return pl.pallas_call(
        functools.partial(_out_kernel, alpha=alpha),
        grid=(n // ROW_TILE,),
        in_specs=[
            pl.BlockSpec((cpt, A_WIDTH, C), lambda t: (t, 0, 0)),
            pl.BlockSpec((cpt, B_WIDTH, C), lambda t: (t, 0, 0)),
            pl.BlockSpec((ROW_TILE, D_MODEL), lambda t: (t, 0)),
            const(w_out), const(ln_g), const(ln_b),
        ],
        out_specs=pl.BlockSpec((ROW_TILE, D_MODEL), lambda t: (t, 0)),
        out_shape=jax.ShapeDtypeStruct((n, D_MODEL), F32),
        compiler_params=pltpu.CompilerParams(dimension_semantics=("parallel",),
                                             vmem_limit_bytes=VMEM_LIMIT),
        name="output",
    )(oaT, obT, x2, w_out, ln_g, ln_b)


def _prepare_weights(w_in, q_norm_g, w_uq, kv_norm_g, w_ukv):
    offs = np.concatenate([[0], np.cumsum(IN_SPLITS)])
    (wqa, wka, wva, wga, wiq, wik, wiw, wcq, wckv, wkr, wgb) = [
        w_in[:, int(offs[j]):int(offs[j + 1])] for j in range(len(IN_SPLITS))]
    d = w_in.shape[0]
    zeros = lambda c: jnp.zeros((d, c), w_in.dtype)
    widx = jnp.concatenate([wiq] + [wik] * (LANES // IDX_DIM) + [wiw, zeros(LANES - IDX_HEADS)], axis=1)
    wkpe = jnp.concatenate([zeros(B_NOPE_DIM), wkr, zeros(LANES - B_NOPE_DIM - B_ROPE_DIM)], axis=1)
    wgT = jnp.concatenate([wga, wgb], axis=1).T

    qk = B_NOPE_DIM + B_ROPE_DIM
    wuq = w_uq.reshape(Q_LORA, B_HEADS, qk)
    wuq = jnp.pad(wuq, ((0, 0), (0, 0), (0, LANES - qk))).reshape(Q_LORA, B_HEADS * LANES)
    wukv = w_ukv.reshape(KV_LORA, B_HEADS, B_NOPE_DIM + B_V_DIM)
    wukvk = jnp.pad(wukv[:, :, :B_NOPE_DIM], ((0, 0), (0, 0), (0, LANES - B_NOPE_DIM)))
    wukvk = wukvk.reshape(KV_LORA, B_HEADS * LANES)
    wukvvT = wukv[:, :, B_NOPE_DIM:].reshape(KV_LORA, B_WIDTH).T

    bf = lambda a: a.astype(BF16)
    order = (np.arange(A_HEADS)[:, None] * A_HEAD_DIM + _head64_order()[None, :]).reshape(-1)
    return [bf(wqa[:, order]), bf(wka[:, order]), widx.astype(F32), bf(wkpe), bf(wcq), bf(wckv),
            bf(wva.T), bf(wgT),
            q_norm_g.reshape(1, Q_LORA).astype(F32), kv_norm_g.reshape(1, KV_LORA).astype(F32),
            bf(wuq), bf(wukvk), bf(wukvvT)]


def kernel(x, positions, w_in, q_norm_g, w_uq, kv_norm_g, w_ukv, w_out, ln_g, ln_b):
    batch, seq, d_model = x.shape
    depth = w_in.shape[0]
    alpha = (2 * depth) ** 0.25
    assert d_model == D_MODEL and seq % ROW_TILE == 0 and ROW_TILE % CHUNK == 0
    n = batch * seq
    pos_col = positions.reshape(n, 1).astype(I32)
    x2 = x.reshape(n, d_model)
    for l in range(depth):
        weights = _prepare_weights(w_in[l], q_norm_g[l], w_uq[l], kv_norm_g[l], w_ukv[l])
        qa, ka, iq, ik, qb, kb, vaT, vbT, gT, iwT = _projection(x2, pos_col, weights)
        oaT = _dsa_attention(iq, iwT, ik, qa, ka, vaT, gT, batch, seq)
        obT = _mla_attention(qb, kb, vbT, gT, batch, seq)
        x2 = _output(oaT, obT, x2, w_out[l].astype(BF16),
                     ln_g[l].reshape(1, d_model), ln_b[l].reshape(1, d_model), alpha)
    return x2.reshape(batch, seq, d_model)
```

```python
import functools

import jax
import jax.numpy as jnp
import numpy as np
from jax import lax
from jax.experimental import pallas as pl
from jax.experimental.pallas import tpu as pltpu

F32 = jnp.float32
BF16 = jnp.bfloat16
I32 = jnp.int32
I16 = jnp.int16

D_MODEL = 1024
A_HEADS = 8
A_HEAD_DIM = 64
A_WIDTH = A_HEADS * A_HEAD_DIM
IDX_HEADS = 8
IDX_DIM = 32
TOPK_MAX = 256
B_HEADS = 8
B_NOPE_DIM = 64
B_ROPE_DIM = 32
B_V_DIM = 64
B_WIDTH = B_HEADS * B_V_DIM
Q_LORA = 256
KV_LORA = 128
ROPE_THETA = 10000.0
LN_EPS = 1e-5
RMS_EPS = 1e-6

IN_SPLITS = (A_WIDTH, A_WIDTH, A_WIDTH, A_WIDTH, IDX_HEADS * IDX_DIM, IDX_DIM, IDX_HEADS,
             Q_LORA, KV_LORA, B_ROPE_DIM, B_WIDTH)

LANES = 128
CHUNK = 256
ROW_TILE = 512
VMEM_LIMIT = 56 * 1024 * 1024
NEG = -1e30
INT_MIN = -2 ** 31
MIN16 = -2 ** 15
PACK = 16
LOG2E = 1.4426950408889634
L_ROWS = 16
LOOKAHEAD = 6

NT_DIMS = (((1,), (1,)), ((), ()))
TN_DIMS = (((0,), (0,)), ((), ()))


def _nt(a, b):
    return lax.dot_general(a, b, NT_DIMS, preferred_element_type=F32)


FREQ_BITS = 12


ROPE_HALF = IDX_DIM // 2


def _head64_order():
    i = np.arange(ROPE_HALF)
    half = A_HEAD_DIM // 2
    return np.concatenate([2 * i, half + 2 * i, 2 * i + 1, half + 2 * i + 1])


def _rope_patterns():
    lane = np.arange(LANES)
    freq_index = 2 * (lane % ROPE_HALF) + (lane // IDX_DIM) % 2
    rest = ROPE_THETA ** (-2.0 * freq_index.astype(np.float64) / A_HEAD_DIM)
    rows = []
    for _ in range(2):
        mant, exp = np.frexp(rest)
        piece = np.ldexp(np.floor(mant * 2 ** FREQ_BITS) / 2 ** FREQ_BITS, exp)
        rows.append(piece)
        rest = rest - piece
    rows.append(rest)
    rows.append(np.where(lane % IDX_DIM < ROPE_HALF, -1.0, 1.0))
    return jnp.asarray(np.stack(rows), F32)


def _rope_tables(pos_ref, pat_ref, lane):
    pos = pos_ref[...].astype(F32)
    a0 = pos * pat_ref[0:1, :]
    dl = pos * pat_ref[1:2, :] + pos * pat_ref[2:3, :]
    c0, s0, cd, sd = jnp.cos(a0), jnp.sin(a0), jnp.cos(dl), jnp.sin(dl)
    c64 = c0 * cd - s0 * sd
    s64 = (s0 * cd + c0 * sd) * pat_ref[3:4, :]
    odd_block = (lane // IDX_DIM) % 2 == 1
    c32 = jnp.where(odd_block, pltpu.roll(c64, IDX_DIM, 1), c64)
    s32 = jnp.where(odd_block, pltpu.roll(s64, IDX_DIM, 1), s64)
    return c64, s64, c32, s32


def _rope_group(xg, cos, sin_signed, low):
    swapped = jnp.where(low, pltpu.roll(xg, LANES - ROPE_HALF, 1), pltpu.roll(xg, ROPE_HALF, 1))
    return xg * cos + swapped * sin_signed


def _proj_kernel(x_ref, pos_ref, pat_ref,
                 wq_ref, wk_ref, widx_ref, wkpe_ref, wcq_ref, wckv_ref,
                 wvT_ref, wgT_ref, gq_ref, gkv_ref, wuq_ref, wukvk_ref, wukvvT_ref,
                 qa_ref, ka_ref, iq_ref, ik_ref, qb_ref, kb_ref,
                 vaT_ref, vbT_ref, gT_ref, iwT_ref):
    tm = x_ref.shape[0]
    xb = x_ref[...].astype(BF16)
    lane = lax.broadcasted_iota(I32, (tm, LANES), 1)
    c64, s64, c32, s32 = _rope_tables(pos_ref, pat_ref, lane)
    low = (lane % IDX_DIM) < ROPE_HALF

    def dot(w_ref):
        return jnp.dot(xb, w_ref[...], preferred_element_type=F32)

    def rope_all(h, cos, sin, out_ref, scale=None, head_dim=None):
        for g in range(h.shape[1] // LANES):
            r = _rope_group(h[:, g * LANES:(g + 1) * LANES], cos, sin, low)
            if scale is not None:
                r = r * scale
            if head_dim is None:
                out_ref[:, g * LANES:(g + 1) * LANES] = r.astype(out_ref.dtype)
                continue
            per_group = LANES // head_dim
            for j in range(per_group):
                o = (g * per_group + j) * LANES
                out_ref[:, o:o + LANES] = jnp.where(lane // head_dim == j, r, 0.0).astype(out_ref.dtype)

    rope_all(dot(wq_ref), c64, s64, qa_ref, A_HEAD_DIM ** -0.5 * LOG2E,
             head_dim=A_HEAD_DIM)
    rope_all(dot(wk_ref), c64, s64, ka_ref)

    hidx = jnp.dot(x_ref[...], widx_ref[...], precision=lax.Precision.HIGHEST,
                   preferred_element_type=F32)
    slot = lane // IDX_DIM

    def split3(v):
        hi = v.astype(BF16).astype(F32)
        mid = (v - hi).astype(BF16).astype(F32)
        lo = (v - hi - mid).astype(BF16).astype(F32)
        return hi, mid, lo

    def idx_rope(g):
        return _rope_group(hidx[:, g * LANES:(g + 1) * LANES], c32, s32, low)

    k_hi, k_mid, k_lo = split3(idx_rope(2))
    ik_ref[:, :LANES] = jnp.where(slot == 1, k_mid, jnp.where(slot == 3, k_lo, k_hi)).astype(BF16)
    ik_ref[:, LANES:] = jnp.where(slot == 0, k_hi, jnp.where(slot == 1, k_mid, 0.0)).astype(BF16)
    slots = LANES // IDX_DIM
    for g in range(IDX_HEADS // slots):
        rolled = [[p if d == 0 else pltpu.roll(p, d * IDX_DIM, 1) for d in range(slots)]
                  for p in split3(idx_rope(g))]
        for s_ in range(slots):
            at = lambda piece, t: rolled[piece][(t - s_) % slots]
            b1 = jnp.where(slot == 0, at(0, 0), jnp.where(slot == 1, at(0, 1),
                           jnp.where(slot == 2, at(1, 2), at(0, 3))))
            b2 = jnp.where(slot == 0, at(2, 0), jnp.where(slot == 1, at(1, 1), 0.0))
            o = (g * slots + s_) * 2 * LANES
            iq_ref[:, o:o + LANES] = b1.astype(BF16)
            iq_ref[:, o + LANES:o + 2 * LANES] = b2.astype(BF16)

    def rmsnorm(c, g_ref):
        ms = jnp.mean(c * c, axis=-1, keepdims=True)
        return (c * lax.rsqrt(ms + RMS_EPS) * g_ref[...]).astype(BF16)

    cqn = rmsnorm(dot(wcq_ref), gq_ref)
    qb = jnp.dot(cqn, wuq_ref[...], preferred_element_type=F32)
    pe_lane = (lane >= B_NOPE_DIM) & (lane < B_NOPE_DIM + B_ROPE_DIM)
    b_scale = (B_NOPE_DIM + B_ROPE_DIM) ** -0.5 * LOG2E
    for h in range(B_HEADS):
        g = qb[:, h * LANES:(h + 1) * LANES]
        r = jnp.where(pe_lane, _rope_group(g, c32, s32, low), g)
        qb_ref[:, h * LANES:(h + 1) * LANES] = (r * b_scale).astype(BF16)

    ckvn = rmsnorm(dot(wckv_ref), gkv_ref)
    kpe = _rope_group(dot(wkpe_ref), c32, s32, low)
    kn = jnp.dot(ckvn, wukvk_ref[...], preferred_element_type=F32)
    for h in range(B_HEADS):
        kb_ref[:, h * LANES:(h + 1) * LANES] = (kn[:, h * LANES:(h + 1) * LANES] + kpe).astype(BF16)

    def store_t(val, out_ref):
        for j in range(tm // CHUNK):
            out_ref[j] = val[:, j * CHUNK:(j + 1) * CHUNK].astype(out_ref.dtype)

    store_t(_nt(wukvvT_ref[...], ckvn), vbT_ref)
    store_t(_nt(wvT_ref[...], xb), vaT_ref)
    gt = _nt(wgT_ref[...], xb)
    store_t(gt * jax.nn.sigmoid(gt), gT_ref)
    iw_t = jnp.transpose(hidx[:, 3 * LANES:4 * LANES])
    store_t(iw_t[:16] * (IDX_DIM ** -0.5 * IDX_HEADS ** -0.5), iwT_ref)


def _projection(x2, pos_col, weights):
    n = x2.shape[0]
    tm = ROW_TILE
    nchunks = n // CHUNK
    cpt = tm // CHUNK

    def rows(width):
        return pl.BlockSpec((tm, width), lambda t: (t, 0))

    def full(a):
        return pl.BlockSpec(a.shape, lambda t: (0,) * a.ndim)

    def tspec(feat):
        return pl.BlockSpec((cpt, feat, CHUNK), lambda t: (t, 0, 0))

    out_shape = [
        jax.ShapeDtypeStruct((n, A_HEADS * LANES), BF16),
        jax.ShapeDtypeStruct((n, A_WIDTH), BF16),
        jax.ShapeDtypeStruct((n, IDX_HEADS * 2 * LANES), BF16),
        jax.ShapeDtypeStruct((n, 2 * LANES), BF16),
        jax.ShapeDtypeStruct((n, B_HEADS * LANES), BF16),
        jax.ShapeDtypeStruct((n, B_HEADS * LANES), BF16),
        jax.ShapeDtypeStruct((nchunks, A_WIDTH, CHUNK), BF16),
        jax.ShapeDtypeStruct((nchunks, B_WIDTH, CHUNK), BF16),
        jax.ShapeDtypeStruct((nchunks, A_WIDTH + B_WIDTH, CHUNK), F32),
        jax.ShapeDtypeStruct((nchunks, 16, CHUNK), F32),
    ]
    out_specs = [rows(A_HEADS * LANES), rows(A_WIDTH), rows(IDX_HEADS * 2 * LANES), rows(2 * LANES),
                 rows(B_HEADS * LANES), rows(B_HEADS * LANES),
                 tspec(A_WIDTH), tspec(B_WIDTH), tspec(A_WIDTH + B_WIDTH), tspec(16)]
    patterns = _rope_patterns()
    in_specs = [rows(D_MODEL), rows(1), full(patterns)] + [full(w) for w in weights]
    return pl.pallas_call(
        _proj_kernel,
        grid=(n // tm,),
        in_specs=in_specs,
        out_specs=out_specs,
        out_shape=out_shape,
        compiler_params=pltpu.CompilerParams(dimension_semantics=("parallel",),
                                             vmem_limit_bytes=VMEM_LIMIT),
        name="projection",
    )(x2, pos_col, patterns, *weights)


def _softmax_step(s, smax, vt, m_ref, acc_ref, h):
    m_old = m_ref[h]
    m_new = jnp.maximum(m_old, smax)
    p = jnp.exp2(s - m_new).astype(BF16)
    vt_ones = jnp.concatenate([vt, jnp.ones((L_ROWS, vt.shape[1]), BF16)], axis=0)
    pv = jnp.dot(vt_ones, p, preferred_element_type=F32)
    acc_ref[h] = jnp.exp2(m_old - m_new) * acc_ref[h] + pv
    m_ref[h] = m_new


def _attention_pipeline(last, heads, logits, consume, s_ref, smax_ref):
    assert LOOKAHEAD < heads

    def issue(c, h):
        s = logits(c, h)
        s_ref[h] = s
        smax_ref[h] = jnp.max(s, axis=0, keepdims=True)

    def step(c, is_last):
        for h in range(heads):
            if h + LOOKAHEAD < heads:
                issue(c, h + LOOKAHEAD)
            elif not is_last:
                issue(c + 1, h + LOOKAHEAD - heads)
            consume(c, h, s_ref[h], smax_ref[h], is_last)

    for h in range(LOOKAHEAD):
        issue(0, h)

    odd = last % 2

    @pl.when(odd == 1)
    def _():
        step(0, False)

    def body(d, carry):
        c = odd + 2 * d
        step(c, False)
        step(c + 1, False)
        return carry

    lax.fori_loop(0, last // 2, body, 0)
    step(last, True)


def _softmax_init(m_ref, acc_ref):
    m_ref[...] = jnp.full(m_ref.shape, NEG, F32)
    acc_ref[...] = jnp.zeros(acc_ref.shape, F32)


def _softmax_finish(acc_ref, g_ref, o_ref, heads, dv):
    for h in range(heads):
        acc = acc_ref[h]
        o = acc[:dv] * (1.0 / acc[dv:dv + 1])
        o_ref[h * dv:(h + 1) * dv, :] = (o * g_ref[h * dv:(h + 1) * dv, :]).astype(o_ref.dtype)


def _dsa_kernel(iq_ref, iwT_ref, ik_ref, qa_ref, ka_ref, vaT_ref, g_ref, o_ref,
                hi_ref, lo_ref, bk_ref, bias_ref, m_ref, acc_ref, s_ref, smax_ref, *, k_top):
    C = CHUNK
    i = pl.program_id(1)
    nk = i + 1
    row = lax.broadcasted_iota(I32, (C, C), 0)
    col = lax.broadcasted_iota(I32, (C, C), 1)
    w = iwT_ref[...]

    def chunk_rows(c):
        return pl.ds(pl.multiple_of(c * C, C), C)

    def score_chunk(c, carry):
        ikc = ik_ref[chunk_rows(c), :]
        acc = jnp.zeros((C, C), F32)
        for h in range(IDX_HEADS):
            lg = _nt(ikc, iq_ref[:, h * 2 * LANES:(h + 1) * 2 * LANES])
            acc = acc + w[h:h + 1, :] * jnp.maximum(lg, 0.0)
        acc = acc + 0.0
        bits = lax.bitcast_convert_type(acc, I32)
        key = jnp.where(bits < 0, bits ^ jnp.int32(0x7FFFFFFF), bits)
        valid = (c * C + row) <= (i * C + col)
        key = jnp.where(valid, key, jnp.int32(INT_MIN))
        hi_ref[chunk_rows(c), :] = (key >> 16).astype(I16)
        lo_ref[chunk_rows(c), :] = (key ^ jnp.int32(0x8000)).astype(I16)
        return carry

    lax.fori_loop(0, nk // 2, lambda d, carry: score_chunk(2 * d + 1, score_chunk(2 * d, carry)), 0)

    @pl.when(nk % 2 == 1)
    def _():
        score_chunk(nk - 1, 0)

    @pl.when(nk % 2 == 1)
    def _():
        hi_ref[chunk_rows(nk), :] = jnp.full((C, C), MIN16, I16)
        lo_ref[chunk_rows(nk), :] = jnp.full((C, C), MIN16, I16)

    n_pairs = (nk + 1) // 2

    def pair_rows(d):
        return pl.ds(pl.multiple_of(d * 2 * C, 2 * C), 2 * C)

    def pack16(v):
        return jnp.broadcast_to(v, (PACK, C)).astype(I16)

    def tiles(x):
        return [x[j * PACK:(j + 1) * PACK] for j in range(x.shape[0] // PACK)]

    def tree_sum(parts):
        while len(parts) > 1:
            parts = [parts[j] + parts[j + 1] for j in range(0, len(parts), 2)]
        return parts[0]

    def count16(ref, pred):
        def body(d, acc):
            hits = [jnp.where(pred(t), jnp.int16(1), jnp.int16(0)) for t in tiles(ref[pair_rows(d), :])]
            return acc + tree_sum(hits)
        acc = lax.fori_loop(0, n_pairs, body, jnp.zeros((PACK, C), I16))
        return jnp.sum(acc.astype(I32), axis=0, keepdims=True)

    def kth_largest(ref, k):
        def bit_body(b, v):
            cand = v + jnp.left_shift(jnp.int32(1), 15 - b)
            c16 = pack16(cand)
            return jnp.where(count16(ref, lambda t: t >= c16) >= k, cand, v)
        return lax.fori_loop(0, 16, bit_body, jnp.full((1, C), MIN16, I32))

    thr_hi = kth_largest(hi_ref, k_top)
    hi16 = pack16(thr_hi)

    def bucket_pair(d, acc):
        his, los = tiles(hi_ref[pair_rows(d), :]), tiles(lo_ref[pair_rows(d), :])
        for j, (th, tl) in enumerate(zip(his, los)):
            bk_ref[pl.ds(pl.multiple_of(d * 2 * C, 2 * C) + j * PACK, PACK), :] = jnp.where(
                th == hi16, tl, jnp.int16(MIN16))
        return acc + tree_sum([jnp.where(th > hi16, jnp.int16(1), jnp.int16(0)) for th in his])

    above = lax.fori_loop(0, n_pairs, bucket_pair, jnp.zeros((PACK, C), I16))
    rank = k_top - jnp.sum(above.astype(I32), axis=0, keepdims=True)
    thr_lo = kth_largest(bk_ref, rank)
    lo16 = pack16(thr_lo)

    need = (rank - count16(bk_ref, lambda t: t > lo16)).astype(F32)
    tri = (row >= col).astype(BF16)

    def bias_chunk(c, carry):
        hi = hi_ref[chunk_rows(c), :].astype(I32)
        lo = lo_ref[chunk_rows(c), :].astype(I32)
        same_hi = hi == thr_hi
        eq = same_hi & (lo == thr_lo)
        gt = (hi > thr_hi) | (same_hi & (lo > thr_lo))
        prefix = jnp.dot(tri, eq.astype(BF16), preferred_element_type=F32) + carry
        sel = gt | (eq & (prefix <= need))
        valid = (c * C + row) <= (i * C + col)
        bias_ref[chunk_rows(c), :] = jnp.where(sel & valid, 0.0, NEG)
        return carry + jnp.sum(eq.astype(F32), axis=0, keepdims=True)

    taken = lax.fori_loop(0, nk // 2, lambda d, carry: bias_chunk(2 * d + 1, bias_chunk(2 * d, carry)),
                          jnp.zeros((1, C), F32))

    @pl.when(nk % 2 == 1)
    def _():
        bias_chunk(nk - 1, taken)

    _softmax_init(m_ref, acc_ref)

    def logits(c, h):
        s = _nt(ka_ref[chunk_rows(c), (h // 2) * LANES:(h // 2 + 1) * LANES],
                qa_ref[:, h * LANES:(h + 1) * LANES])
        return s + bias_ref[chunk_rows(c), :]

    def consume(c, h, s, smax, is_last):
        _softmax_step(s, smax, vaT_ref[c, h * A_HEAD_DIM:(h + 1) * A_HEAD_DIM, :], m_ref, acc_ref, h)

    _attention_pipeline(i, A_HEADS, logits, consume, s_ref, smax_ref)
    _softmax_finish(acc_ref, g_ref, o_ref, A_HEADS, A_HEAD_DIM)


def _dsa_attention(iq, iwT, ik, qa, ka, vaT, gT, batch, seq):
    C = CHUNK
    nc = seq // C
    k_top = min(TOPK_MAX, seq // 4)
    sel_rows = (nc + nc % 2) * C
    return pl.pallas_call(
        functools.partial(_dsa_kernel, k_top=k_top),
        grid=(batch, nc),
        in_specs=[
            pl.BlockSpec((C, IDX_HEADS * 2 * LANES), lambda b, i: (b * nc + i, 0)),
            pl.BlockSpec((None, 16, C), lambda b, i: (b * nc + i, 0, 0)),
            pl.BlockSpec((seq, 2 * LANES), lambda b, i: (b, 0)),
            pl.BlockSpec((C, A_HEADS * LANES), lambda b, i: (b * nc + i, 0)),
            pl.BlockSpec((seq, A_WIDTH), lambda b, i: (b, 0)),
            pl.BlockSpec((nc, A_WIDTH, C), lambda b, i: (b, 0, 0)),
            pl.BlockSpec((None, A_WIDTH, C), lambda b, i: (b * nc + i, 0, 0)),
        ],
        out_specs=pl.BlockSpec((None, A_WIDTH, C), lambda b, i: (b * nc + i, 0, 0)),
        out_shape=jax.ShapeDtypeStruct((batch * nc, A_WIDTH, C), BF16),
        scratch_shapes=[
            pltpu.VMEM((sel_rows, C), I16),
            pltpu.VMEM((sel_rows, C), I16),
            pltpu.VMEM((sel_rows, C), I16),
            pltpu.VMEM((seq, C), F32),
            pltpu.VMEM((A_HEADS, 1, C), F32),
            pltpu.VMEM((A_HEADS, A_HEAD_DIM + L_ROWS, C), F32),
            pltpu.VMEM((A_HEADS, C, C), F32),
            pltpu.VMEM((A_HEADS, 1, C), F32),
        ],
        compiler_params=pltpu.CompilerParams(dimension_semantics=("parallel", "arbitrary"),
                                             vmem_limit_bytes=VMEM_LIMIT),
        name="dsa_attention",
    )(iq, iwT, ik, qa, ka, vaT, gT)


def _mla_kernel(qb_ref, kb_ref, vbT_ref, g_ref, o_ref, m_ref, acc_ref, s_ref, smax_ref):
    C = CHUNK
    i = pl.program_id(1)
    row = lax.broadcasted_iota(I32, (C, C), 0)
    col = lax.broadcasted_iota(I32, (C, C), 1)
    causal = row <= col
    _softmax_init(m_ref, acc_ref)

    def logits(c, h):
        kc = kb_ref[pl.ds(pl.multiple_of(c * C, C), C), h * LANES:(h + 1) * LANES]
        return _nt(kc, qb_ref[:, h * LANES:(h + 1) * LANES])

    def consume(c, h, s, smax, is_last):
        if is_last:
            s = jnp.where(causal, s, NEG)
            smax = jnp.max(s, axis=0, keepdims=True)
        _softmax_step(s, smax, vbT_ref[c, h * B_V_DIM:(h + 1) * B_V_DIM, :], m_ref, acc_ref, h)

    _attention_pipeline(i, B_HEADS, logits, consume, s_ref, smax_ref)
    _softmax_finish(acc_ref, g_ref, o_ref, B_HEADS, B_V_DIM)


def _mla_attention(qb, kb, vbT, gT, batch, seq):
    C = CHUNK
    nc = seq // C
    return pl.pallas_call(
        _mla_kernel,
        grid=(batch, nc),
        in_specs=[
            pl.BlockSpec((C, B_HEADS * LANES), lambda b, i: (b * nc + i, 0)),
            pl.BlockSpec((seq, B_HEADS * LANES), lambda b, i: (b, 0)),
            pl.BlockSpec((nc, B_WIDTH, C), lambda b, i: (b, 0, 0)),
            pl.BlockSpec((None, B_WIDTH, C), lambda b, i: (b * nc + i, 1, 0)),
        ],
        out_specs=pl.BlockSpec((None, B_WIDTH, C), lambda b, i: (b * nc + i, 0, 0)),
        out_shape=jax.ShapeDtypeStruct((batch * nc, B_WIDTH, C), BF16),
        scratch_shapes=[
            pltpu.VMEM((B_HEADS, 1, C), F32),
            pltpu.VMEM((B_HEADS, B_V_DIM + L_ROWS, C), F32),
            pltpu.VMEM((B_HEADS, C, C), F32),
            pltpu.VMEM((B_HEADS, 1, C), F32),
        ],
        compiler_params=pltpu.CompilerParams(dimension_semantics=("parallel", "arbitrary"),
                                             vmem_limit_bytes=VMEM_LIMIT),
        name="mla_attention",
    )(qb, kb, vbT, gT)


def _out_kernel(oa_ref, ob_ref, x_ref, w_ref, lng_ref, lnb_ref, y_ref, *, alpha):
    for j in range(oa_ref.shape[0]):
        rows = slice(j * CHUNK, (j + 1) * CHUNK)
        out = (lax.dot_general(oa_ref[j], w_ref[:A_WIDTH, :], TN_DIMS, preferred_element_type=F32)
               + lax.dot_general(ob_ref[j], w_ref[A_WIDTH:, :], TN_DIMS, preferred_element_type=F32))
        z = alpha * x_ref[rows, :] + out
        mu = jnp.mean(z, axis=-1, keepdims=True)
        zc = z - mu
        var = jnp.mean(zc * zc, axis=-1, keepdims=True)
        y_ref[rows, :] = zc * lax.rsqrt(var + LN_EPS) * lng_ref[...] + lnb_ref[...]


def _output(oaT, obT, x2, w_out, ln_g, ln_b, alpha):
    C = CHUNK
    n = x2.shape[0]
    cpt = ROW_TILE // C
    const = lambda a: pl.BlockSpec(a.shape, lambda t: (0,) * a.ndim)
    return pl.pallas_call(
        functools.partial(_out_kernel, alpha=alpha),
        grid=(n // ROW_TILE,),
        in_specs=[
            pl.BlockSpec((cpt, A_WIDTH, C), lambda t: (t, 0, 0)),
            pl.BlockSpec((cpt, B_WIDTH, C), lambda t: (t, 0, 0)),
            pl.BlockSpec((ROW_TILE, D_MODEL), lambda t: (t, 0)),
            const(w_out), const(ln_g), const(ln_b),
        ],
        out_specs=pl.BlockSpec((ROW_TILE, D_MODEL), lambda t: (t, 0)),
        out_shape=jax.ShapeDtypeStruct((n, D_MODEL), F32),
        compiler_params=pltpu.CompilerParams(dimension_semantics=("parallel",),
                                             vmem_limit_bytes=VMEM_LIMIT),
        name="output",
    )(oaT, obT, x2, w_out, ln_g, ln_b)


def _prepare_weights(w_in, q_norm_g, w_uq, kv_norm_g, w_ukv):
    offs = np.concatenate([[0], np.cumsum(IN_SPLITS)])
    (wqa, wka, wva, wga, wiq, wik, wiw, wcq, wckv, wkr, wgb) = [
        w_in[:, int(offs[j]):int(offs[j + 1])] for j in range(len(IN_SPLITS))]
    d = w_in.shape[0]
    zeros = lambda c: jnp.zeros((d, c), w_in.dtype)
    widx = jnp.concatenate([wiq] + [wik] * (LANES // IDX_DIM) + [wiw, zeros(LANES - IDX_HEADS)], axis=1)
    wkpe = jnp.concatenate([zeros(B_NOPE_DIM), wkr, zeros(LANES - B_NOPE_DIM - B_ROPE_DIM)], axis=1)
    wgT = jnp.concatenate([wga, wgb], axis=1).T

    qk = B_NOPE_DIM + B_ROPE_DIM
    wuq = w_uq.reshape(Q_LORA, B_HEADS, qk)
    wuq = jnp.pad(wuq, ((0, 0), (0, 0), (0, LANES - qk))).reshape(Q_LORA, B_HEADS * LANES)
    wukv = w_ukv.reshape(KV_LORA, B_HEADS, B_NOPE_DIM + B_V_DIM)
    wukvk = jnp.pad(wukv[:, :, :B_NOPE_DIM], ((0, 0), (0, 0), (0, LANES - B_NOPE_DIM)))
    wukvk = wukvk.reshape(KV_LORA, B_HEADS * LANES)
    wukvvT = wukv[:, :, B_NOPE_DIM:].reshape(KV_LORA, B_WIDTH).T

    bf = lambda a: a.astype(BF16)
    order = (np.arange(A_HEADS)[:, None] * A_HEAD_DIM + _head64_order()[None, :]).reshape(-1)
    return [bf(wqa[:, order]), bf(wka[:, order]), widx.astype(F32), bf(wkpe), bf(wcq), bf(wckv),
            bf(wva.T), bf(wgT),
            q_norm_g.reshape(1, Q_LORA).astype(F32), kv_norm_g.reshape(1, KV_LORA).astype(F32),
            bf(wuq), bf(wukvk), bf(wukvvT)]


def kernel(x, positions, w_in, q_norm_g, w_uq, kv_norm_g, w_ukv, w_out, ln_g, ln_b):
    batch, seq, d_model = x.shape
    depth = w_in.shape[0]
    alpha = (2 * depth) ** 0.25
    assert d_model == D_MODEL and seq % ROW_TILE == 0 and ROW_TILE % CHUNK == 0
    n = batch * seq
    pos_col = positions.reshape(n, 1).astype(I32)
    x2 = x.reshape(n, d_model)
    for l in range(depth):
        weights = _prepare_weights(w_in[l], q_norm_g[l], w_uq[l], kv_norm_g[l], w_ukv[l])
        qa, ka, iq, ik, qb, kb, vaT, vbT, gT, iwT = _projection(x2, pos_col, weights)
        oaT = _dsa_attention(iq, iwT, ik, qa, ka, vaT, gT, batch, seq)
        obT = _mla_attention(qb, kb, vbT, gT, batch, seq)
        x2 = _output(oaT, obT, x2, w_out[l].astype(BF16),
                     ln_g[l].reshape(1, d_model), ln_b[l].reshape(1, d_model), alpha)
    return x2.reshape(batch, seq, d_model)
```

```python
import functools

import jax
import jax.numpy as jnp
import numpy as np
from jax import lax
from jax.experimental import pallas as pl
from jax.experimental.pallas import tpu as pltpu

F32 = jnp.float32
BF16 = jnp.bfloat16
I32 = jnp.int32
I16 = jnp.int16

D_MODEL = 1024
A_HEADS = 8
A_HEAD_DIM = 64
A_WIDTH = A_HEADS * A_HEAD_DIM
IDX_HEADS = 8
IDX_DIM = 32
TOPK_MAX = 256
B_HEADS = 8
B_NOPE_DIM = 64
B_ROPE_DIM = 32
B_V_DIM = 64
B_WIDTH = B_HEADS * B_V_DIM
Q_LORA = 256
KV_LORA = 128
ROPE_THETA = 10000.0
LN_EPS = 1e-5
RMS_EPS = 1e-6

IN_SPLITS = (A_WIDTH, A_WIDTH, A_WIDTH, A_WIDTH, IDX_HEADS * IDX_DIM, IDX_DIM, IDX_HEADS,
             Q_LORA, KV_LORA, B_ROPE_DIM, B_WIDTH)

LANES = 128
CHUNK = 256
ROW_TILE = 512
VMEM_LIMIT = 56 * 1024 * 1024
NEG = -1e30
INT_MIN = -2 ** 31
MIN16 = -2 ** 15
PACK = 16
LOG2E = 1.4426950408889634
L_ROWS = 16
LOOKAHEAD = 6

NT_DIMS = (((1,), (1,)), ((), ()))
TN_DIMS = (((0,), (0,)), ((), ()))


def _nt(a, b):
    return lax.dot_general(a, b, NT_DIMS, preferred_element_type=F32)


FREQ_BITS = 12


ROPE_HALF = IDX_DIM // 2


def _head64_order():
    i = np.arange(ROPE_HALF)
    half = A_HEAD_DIM // 2
    return np.concatenate([2 * i, half + 2 * i, 2 * i + 1, half + 2 * i + 1])


def _rope_patterns():
    lane = np.arange(LANES)
    j = lane % IDX_DIM
    freq_index = np.where(j < ROPE_HALF, 2 * j, 2 * (j - ROPE_HALF) + 1)
    rest = ROPE_THETA ** (-2.0 * freq_index.astype(np.float64) / A_HEAD_DIM)
    rows = []
    for _ in range(2):
        mant, exp = np.frexp(rest)
        piece = np.ldexp(np.floor(mant * 2 ** FREQ_BITS) / 2 ** FREQ_BITS, exp)
        rows.append(piece)
        rest = rest - piece
    rows.append(rest)
    rows.append(np.where(lane % IDX_DIM < ROPE_HALF, -1.0, 1.0))
    return jnp.asarray(np.stack(rows), F32)


def _rope_tables(pos_ref, pat_ref):
    blocks = LANES // IDX_DIM
    assert pos_ref.shape[0] == blocks * LANES
    lane = lax.broadcasted_iota(I32, (LANES, LANES), 1)
    blk = lane // IDX_DIM
    unit = lane // ROPE_HALF
    pos = pos_ref[...].astype(F32)
    p = jnp.broadcast_to(pos[(blocks - 1) * LANES:], (LANES, LANES))
    for b in range(blocks - 1):
        p = jnp.where(blk == b, pos[b * LANES:(b + 1) * LANES], p)
    a0 = p * pat_ref[0:1, :]
    dl = p * pat_ref[1:2, :] + p * pat_ref[2:3, :]
    c0, s0, cd, sd = jnp.cos(a0), jnp.sin(a0), jnp.cos(dl), jnp.sin(dl)

    def spread(x):
        rolled = [x] + [pltpu.roll(x, IDX_DIM * k, 1) for k in range(1, blocks)]
        wide, narrow = [], []
        for b in range(blocks):
            y = rolled[(blocks - 1 - b) % blocks]
            for t in range(blocks - 1):
                y = jnp.where(blk == t, rolled[(t - b) % blocks], y)
            r = pltpu.roll(y, ROPE_HALF, 1)
            wide.append(jnp.where((unit % 4 == 1) | (unit % 4 == 2), r, y))
            narrow.append(jnp.where(unit % 2 == 1, r, y))
        return jnp.concatenate(wide, axis=0), jnp.concatenate(narrow, axis=0)

    c64, c32 = spread(c0 * cd - s0 * sd)
    s64, s32 = spread(s0 * cd + c0 * sd)
    sign = pat_ref[3:4, :]
    return c64, s64 * sign, c32, s32 * sign


def _rope_group(xg, cos, sin_signed, low):
    swapped = jnp.where(low, pltpu.roll(xg, LANES - ROPE_HALF, 1), pltpu.roll(xg, ROPE_HALF, 1))
    return xg * cos + swapped * sin_signed


def _proj_kernel(x_ref, pos_ref, pat_ref,
                 wq_ref, wk_ref, widx_ref, wkpe_ref, wcq_ref, wckv_ref,
                 wvT_ref, wgT_ref, gq_ref, gkv_ref, wuq_ref, wukvk_ref, wukvvT_ref,
                 qa_ref, ka_ref, iq_ref, ik_ref, qb_ref, kb_ref,
                 vaT_ref, vbT_ref, gT_ref, iwT_ref):
    tm = x_ref.shape[0]
    xb = x_ref[...].astype(BF16)
    lane = lax.broadcasted_iota(I32, (tm, LANES), 1)
    c64, s64, c32, s32 = _rope_tables(pos_ref, pat_ref)
    low = (lane % IDX_DIM) < ROPE_HALF

    def dot(w_ref):
        return jnp.dot(xb, w_ref[...], preferred_element_type=F32)

    def rope_all(h, cos, sin, out_ref, scale=None, head_dim=None):
        for g in range(h.shape[1] // LANES):
            r = _rope_group(h[:, g * LANES:(g + 1) * LANES], cos, sin, low)
            if scale is not None:
                r = r * scale
            if head_dim is None:
                out_ref[:, g * LANES:(g + 1) * LANES] = r.astype(out_ref.dtype)
                continue
            per_group = LANES // head_dim
            for j in range(per_group):
                o = (g * per_group + j) * LANES
                out_ref[:, o:o + LANES] = jnp.where(lane // head_dim == j, r, 0.0).astype(out_ref.dtype)

    rope_all(dot(wq_ref), c64, s64, qa_ref, A_HEAD_DIM ** -0.5 * LOG2E,
             head_dim=A_HEAD_DIM)
    rope_all(dot(wk_ref), c64, s64, ka_ref)

    hidx = jnp.dot(x_ref[...], widx_ref[...], precision=lax.Precision.HIGHEST,
                   preferred_element_type=F32)
    slot = lane // IDX_DIM

    def split3(v):
        hi = v.astype(BF16).astype(F32)
        mid = (v - hi).astype(BF16).astype(F32)
        lo = (v - hi - mid).astype(BF16).astype(F32)
        return hi, mid, lo

    def idx_rope(g):
        return _rope_group(hidx[:, g * LANES:(g + 1) * LANES], c32, s32, low)

    k_hi, k_mid, k_lo = split3(idx_rope(2))
    ik_ref[:, :LANES] = jnp.where(slot == 1, k_mid, jnp.where(slot == 3, k_lo, k_hi)).astype(BF16)
    ik_ref[:, LANES:] = jnp.where(slot == 0, k_hi, jnp.where(slot == 1, k_mid, 0.0)).astype(BF16)
    slots = LANES // IDX_DIM
    for g in range(IDX_HEADS // slots):
        rolled = [[p if d == 0 else pltpu.roll(p, d * IDX_DIM, 1) for d in range(slots)]
                  for p in split3(idx_rope(g))]
        for s_ in range(slots):
            at = lambda piece, t: rolled[piece][(t - s_) % slots]
            b1 = jnp.where(slot == 0, at(0, 0), jnp.where(slot == 1, at(0, 1),
                           jnp.where(slot == 2, at(1, 2), at(0, 3))))
            b2 = jnp.where(slot == 0, at(2, 0), jnp.where(slot == 1, at(1, 1), 0.0))
            o = (g * slots + s_) * 2 * LANES
            iq_ref[:, o:o + LANES] = b1.astype(BF16)
            iq_ref[:, o + LANES:o + 2 * LANES] = b2.astype(BF16)

    def rmsnorm(c, g_ref):
        ms = jnp.mean(c * c, axis=-1, keepdims=True)
        return (c * lax.rsqrt(ms + RMS_EPS) * g_ref[...]).astype(BF16)

    cqn = rmsnorm(dot(wcq_ref), gq_ref)
    qb = jnp.dot(cqn, wuq_ref[...], preferred_element_type=F32)
    pe_lane = (lane >= B_NOPE_DIM) & (lane < B_NOPE_DIM + B_ROPE_DIM)
    b_scale = (B_NOPE_DIM + B_ROPE_DIM) ** -0.5 * LOG2E
    for h in range(B_HEADS):
        g = qb[:, h * LANES:(h + 1) * LANES]
        r = jnp.where(pe_lane, _rope_group(g, c32, s32, low), g)
        qb_ref[:, h * LANES:(h + 1) * LANES] = (r * b_scale).astype(BF16)

    ckvn = rmsnorm(dot(wckv_ref), gkv_ref)
    kpe = _rope_group(dot(wkpe_ref), c32, s32, low)
    kn = jnp.dot(ckvn, wukvk_ref[...], preferred_element_type=F32)
    for h in range(B_HEADS):
        kb_ref[:, h * LANES:(h + 1) * LANES] = (kn[:, h * LANES:(h + 1) * LANES] + kpe).astype(BF16)

    def store_t(val, out_ref):
        for j in range(tm // CHUNK):
            out_ref[j] = val[:, j * CHUNK:(j + 1) * CHUNK].astype(out_ref.dtype)

    store_t(_nt(wukvvT_ref[...], ckvn), vbT_ref)
    store_t(_nt(wvT_ref[...], xb), vaT_ref)
    gt = _nt(wgT_ref[...], xb)
    store_t(gt * jax.nn.sigmoid(gt), gT_ref)
    iw_t = jnp.transpose(hidx[:, 3 * LANES:4 * LANES])
    store_t(iw_t[:16] * (IDX_DIM ** -0.5 * IDX_HEADS ** -0.5), iwT_ref)


def _projection(x2, pos_col, weights):
    n = x2.shape[0]
    tm = ROW_TILE
    nchunks = n // CHUNK
    cpt = tm // CHUNK

    def rows(width):
        return pl.BlockSpec((tm, width), lambda t: (t, 0))

    def full(a):
        return pl.BlockSpec(a.shape, lambda t: (0,) * a.ndim)

    def tspec(feat):
        return pl.BlockSpec((cpt, feat, CHUNK), lambda t: (t, 0, 0))

    out_shape = [
        jax.ShapeDtypeStruct((n, A_HEADS * LANES), BF16),
        jax.ShapeDtypeStruct((n, A_WIDTH), BF16),
        jax.ShapeDtypeStruct((n, IDX_HEADS * 2 * LANES), BF16),
        jax.ShapeDtypeStruct((n, 2 * LANES), BF16),
        jax.ShapeDtypeStruct((n, B_HEADS * LANES), BF16),
        jax.ShapeDtypeStruct((n, B_HEADS * LANES), BF16),
        jax.ShapeDtypeStruct((nchunks, A_WIDTH, CHUNK), BF16),
        jax.ShapeDtypeStruct((nchunks, B_WIDTH, CHUNK), BF16),
        jax.ShapeDtypeStruct((nchunks, A_WIDTH + B_WIDTH, CHUNK), F32),
        jax.ShapeDtypeStruct((nchunks, 16, CHUNK), F32),
    ]
    out_specs = [rows(A_HEADS * LANES), rows(A_WIDTH), rows(IDX_HEADS * 2 * LANES), rows(2 * LANES),
                 rows(B_HEADS * LANES), rows(B_HEADS * LANES),
                 tspec(A_WIDTH), tspec(B_WIDTH), tspec(A_WIDTH + B_WIDTH), tspec(16)]
    patterns = _rope_patterns()
    in_specs = [rows(D_MODEL), rows(1), full(patterns)] + [full(w) for w in weights]
    return pl.pallas_call(
        _proj_kernel,
        grid=(n // tm,),
        in_specs=in_specs,
        out_specs=out_specs,
        out_shape=out_shape,
        compiler_params=pltpu.CompilerParams(dimension_semantics=("parallel",),
                                             vmem_limit_bytes=VMEM_LIMIT),
        name="projection",
    )(x2, pos_col, patterns, *weights)


def _softmax_step(s, smax, vt, m_ref, acc_ref, h):
    m_old = m_ref[h]
    m_new = jnp.maximum(m_old, smax)
    p = jnp.exp2(s - m_new).astype(BF16)
    vt_ones = jnp.concatenate([vt, jnp.ones((L_ROWS, vt.shape[1]), BF16)], axis=0)
    pv = jnp.dot(vt_ones, p, preferred_element_type=F32)
    acc_ref[h] = jnp.exp2(m_old - m_new) * acc_ref[h] + pv
    m_ref[h] = m_new


def _attention_pipeline(last, heads, logits, consume, s_ref, smax_ref):
    assert LOOKAHEAD < heads

    def issue(c, h):
        s = logits(c, h)
        s_ref[h] = s
        smax_ref[h] = jnp.max(s, axis=0, keepdims=True)

    def step(c, is_last):
        for h in range(heads):
            if h + LOOKAHEAD < heads:
                issue(c, h + LOOKAHEAD)
            elif not is_last:
                issue(c + 1, h + LOOKAHEAD - heads)
            consume(c, h, s_ref[h], smax_ref[h], is_last)

    for h in range(LOOKAHEAD):
        issue(0, h)

    odd = last % 2

    @pl.when(odd == 1)
    def _():
        step(0, False)

    def body(d, carry):
        c = odd + 2 * d
        step(c, False)
        step(c + 1, False)
        return carry

    lax.fori_loop(0, last // 2, body, 0)
    step(last, True)


def _softmax_init(m_ref, acc_ref):
    m_ref[...] = jnp.full(m_ref.shape, NEG, F32)
    acc_ref[...] = jnp.zeros(acc_ref.shape, F32)


def _softmax_finish(acc_ref, g_ref, o_ref, heads, dv):
    for h in range(heads):
        acc = acc_ref[h]
        o = acc[:dv] * (1.0 / acc[dv:dv + 1])
        o_ref[h * dv:(h + 1) * dv, :] = (o * g_ref[h * dv:(h + 1) * dv, :]).astype(o_ref.dtype)


def _dsa_kernel(iq_ref, iwT_ref, ik_ref, qa_ref, ka_ref, vaT_ref, g_ref, o_ref,
                hi_ref, lo_ref, bk_ref, bias_ref, m_ref, acc_ref, s_ref, smax_ref, *, k_top):
    C = CHUNK
    i = pl.program_id(1)
    nk = i + 1
    row = lax.broadcasted_iota(I32, (C, C), 0)
    col = lax.broadcasted_iota(I32, (C, C), 1)
    w = iwT_ref[...]

    def chunk_rows(c):
        return pl.ds(pl.multiple_of(c * C, C), C)

    def score_chunk(c, carry):
        ikc = ik_ref[chunk_rows(c), :]
        acc = jnp.zeros((C, C), F32)
        for h in range(IDX_HEADS):
            lg = _nt(ikc, iq_ref[:, h * 2 * LANES:(h + 1) * 2 * LANES])
            acc = acc + w[h:h + 1, :] * jnp.maximum(lg, 0.0)
        acc = acc + 0.0
        bits = lax.bitcast_convert_type(acc, I32)
        key = jnp.where(bits < 0, bits ^ jnp.int32(0x7FFFFFFF), bits)
        valid = (c * C + row) <= (i * C + col)
        key = jnp.where(valid, key, jnp.int32(INT_MIN))
        hi_ref[chunk_rows(c), :] = (key >> 16).astype(I16)
        lo_ref[chunk_rows(c), :] = (key ^ jnp.int32(0x8000)).astype(I16)
        return carry

    lax.fori_loop(0, nk // 2, lambda d, carry: score_chunk(2 * d + 1, score_chunk(2 * d, carry)), 0)

    @pl.when(nk % 2 == 1)
    def _():
        score_chunk(nk - 1, 0)

    @pl.when(nk % 2 == 1)
    def _():
        hi_ref[chunk_rows(nk), :] = jnp.full((C, C), MIN16, I16)
        lo_ref[chunk_rows(nk), :] = jnp.full((C, C), MIN16, I16)

    n_pairs = (nk + 1) // 2

    def pair_rows(d):
        return pl.ds(pl.multiple_of(d * 2 * C, 2 * C), 2 * C)

    def pack16(v):
        return jnp.broadcast_to(v, (PACK, C)).astype(I16)

    def tiles(x):
        return [x[j * PACK:(j + 1) * PACK] for j in range(x.shape[0] // PACK)]

    def tree_sum(parts):
        while len(parts) > 1:
            parts = [parts[j] + parts[j + 1] for j in range(0, len(parts), 2)]
        return parts[0]

    def count16(ref, pred):
        def body(d, acc):
            hits = [jnp.where(pred(t), jnp.int16(1), jnp.int16(0)) for t in tiles(ref[pair_rows(d), :])]
            return acc + tree_sum(hits)
        acc = lax.fori_loop(0, n_pairs, body, jnp.zeros((PACK, C), I16))
        return jnp.sum(acc.astype(I32), axis=0, keepdims=True)

    def kth_largest(ref, k):
        def bit_body(b, v):
            cand = v + jnp.left_shift(jnp.int32(1), 15 - b)
            c16 = pack16(cand)
            return jnp.where(count16(ref, lambda t: t >= c16) >= k, cand, v)
        return lax.fori_loop(0, 16, bit_body, jnp.full((1, C), MIN16, I32))

    thr_hi = kth_largest(hi_ref, k_top)
    hi16 = pack16(thr_hi)

    def bucket_pair(d, acc):
        his, los = tiles(hi_ref[pair_rows(d), :]), tiles(lo_ref[pair_rows(d), :])
        for j, (th, tl) in enumerate(zip(his, los)):
            bk_ref[pl.ds(pl.multiple_of(d * 2 * C, 2 * C) + j * PACK, PACK), :] = jnp.where(
                th == hi16, tl, jnp.int16(MIN16))
        return acc + tree_sum([jnp.where(th > hi16, jnp.int16(1), jnp.int16(0)) for th in his])

    above = lax.fori_loop(0, n_pairs, bucket_pair, jnp.zeros((PACK, C), I16))
    rank = k_top - jnp.sum(above.astype(I32), axis=0, keepdims=True)
    thr_lo = kth_largest(bk_ref, rank)
    lo16 = pack16(thr_lo)

    need = (rank - count16(bk_ref, lambda t: t > lo16)).astype(F32)
    tri = (row >= col).astype(BF16)

    def bias_chunk(c, carry):
        hi = hi_ref[chunk_rows(c), :].astype(I32)
        lo = lo_ref[chunk_rows(c), :].astype(I32)
        same_hi = hi == thr_hi
        eq = same_hi & (lo == thr_lo)
        gt = (hi > thr_hi) | (same_hi & (lo > thr_lo))
        prefix = jnp.dot(tri, eq.astype(BF16), preferred_element_type=F32) + carry
        sel = gt | (eq & (prefix <= need))
        valid = (c * C + row) <= (i * C + col)
        bias_ref[chunk_rows(c), :] = jnp.where(sel & valid, 0.0, NEG)
        return carry + jnp.sum(eq.astype(F32), axis=0, keepdims=True)

    taken = lax.fori_loop(0, nk // 2, lambda d, carry: bias_chunk(2 * d + 1, bias_chunk(2 * d, carry)),
                          jnp.zeros((1, C), F32))

    @pl.when(nk % 2 == 1)
    def _():
        bias_chunk(nk - 1, taken)

    _softmax_init(m_ref, acc_ref)

    def logits(c, h):
        s = _nt(ka_ref[chunk_rows(c), (h // 2) * LANES:(h // 2 + 1) * LANES],
                qa_ref[:, h * LANES:(h + 1) * LANES])
        return s + bias_ref[chunk_rows(c), :]

    def consume(c, h, s, smax, is_last):
        _softmax_step(s, smax, vaT_ref[c, h * A_HEAD_DIM:(h + 1) * A_HEAD_DIM, :], m_ref, acc_ref, h)

    _attention_pipeline(i, A_HEADS, logits, consume, s_ref, smax_ref)
    _softmax_finish(acc_ref, g_ref, o_ref, A_HEADS, A_HEAD_DIM)


def _dsa_attention(iq, iwT, ik, qa, ka, vaT, gT, batch, seq):
    C = CHUNK
    nc = seq // C
    k_top = min(TOPK_MAX, seq // 4)
    sel_rows = (nc + nc % 2) * C
    return pl.pallas_call(
        functools.partial(_dsa_kernel, k_top=k_top),
        grid=(batch, nc),
        in_specs=[
            pl.BlockSpec((C, IDX_HEADS * 2 * LANES), lambda b, i: (b * nc + i, 0)),
            pl.BlockSpec((None, 16, C), lambda b, i: (b * nc + i, 0, 0)),
            pl.BlockSpec((seq, 2 * LANES), lambda b, i: (b, 0)),
            pl.BlockSpec((C, A_HEADS * LANES), lambda b, i: (b * nc + i, 0)),
            pl.BlockSpec((seq, A_WIDTH), lambda b, i: (b, 0)),
            pl.BlockSpec((nc, A_WIDTH, C), lambda b, i: (b, 0, 0)),
            pl.BlockSpec((None, A_WIDTH, C), lambda b, i: (b * nc + i, 0, 0)),
        ],
        out_specs=pl.BlockSpec((None, A_WIDTH, C), lambda b, i: (b * nc + i, 0, 0)),
        out_shape=jax.ShapeDtypeStruct((batch * nc, A_WIDTH, C), BF16),
        scratch_shapes=[
            pltpu.VMEM((sel_rows, C), I16),
            pltpu.VMEM((sel_rows, C), I16),
            pltpu.VMEM((sel_rows, C), I16),
            pltpu.VMEM((seq, C), F32),
            pltpu.VMEM((A_HEADS, 1, C), F32),
            pltpu.VMEM((A_HEADS, A_HEAD_DIM + L_ROWS, C), F32),
            pltpu.VMEM((A_HEADS, C, C), F32),
            pltpu.VMEM((A_HEADS, 1, C), F32),
        ],
        compiler_params=pltpu.CompilerParams(dimension_semantics=("parallel", "arbitrary"),
                                             vmem_limit_bytes=VMEM_LIMIT),
        name="dsa_attention",
    )(iq, iwT, ik, qa, ka, vaT, gT)


def _mla_kernel(qb_ref, kb_ref, vbT_ref, g_ref, o_ref, m_ref, acc_ref, s_ref, smax_ref):
    C = CHUNK
    i = pl.program_id(1)
    row = lax.broadcasted_iota(I32, (C, C), 0)
    col = lax.broadcasted_iota(I32, (C, C), 1)
    causal = row <= col
    _softmax_init(m_ref, acc_ref)

    def logits(c, h):
        kc = kb_ref[pl.ds(pl.multiple_of(c * C, C), C), h * LANES:(h + 1) * LANES]
        return _nt(kc, qb_ref[:, h * LANES:(h + 1) * LANES])

    def consume(c, h, s, smax, is_last):
        if is_last:
            s = jnp.where(causal, s, NEG)
            smax = jnp.max(s, axis=0, keepdims=True)
        _softmax_step(s, smax, vbT_ref[c, h * B_V_DIM:(h + 1) * B_V_DIM, :], m_ref, acc_ref, h)

    _attention_pipeline(i, B_HEADS, logits, consume, s_ref, smax_ref)
    _softmax_finish(acc_ref, g_ref, o_ref, B_HEADS, B_V_DIM)


def _mla_attention(qb, kb, vbT, gT, batch, seq):
    C = CHUNK
    nc = seq // C
    return pl.pallas_call(
        _mla_kernel,
        grid=(batch, nc),
        in_specs=[
            pl.BlockSpec((C, B_HEADS * LANES), lambda b, i: (b * nc + i, 0)),
            pl.BlockSpec((seq, B_HEADS * LANES), lambda b, i: (b, 0)),
            pl.BlockSpec((nc, B_WIDTH, C), lambda b, i: (b, 0, 0)),
            pl.BlockSpec((None, B_WIDTH, C), lambda b, i: (b * nc + i, 1, 0)),
        ],
        out_specs=pl.BlockSpec((None, B_WIDTH, C), lambda b, i: (b * nc + i, 0, 0)),
        out_shape=jax.ShapeDtypeStruct((batch * nc, B_WIDTH, C), BF16),
        scratch_shapes=[
            pltpu.VMEM((B_HEADS, 1, C), F32),
            pltpu.VMEM((B_HEADS, B_V_DIM + L_ROWS, C), F32),
            pltpu.VMEM((B_HEADS, C, C), F32),
            pltpu.VMEM((B_HEADS, 1, C), F32),
        ],
        compiler_params=pltpu.CompilerParams(dimension_semantics=("parallel", "arbitrary"),
                                             vmem_limit_bytes=VMEM_LIMIT),
        name="mla_attention",
    )(qb, kb, vbT, gT)


def _out_kernel(oa_ref, ob_ref, x_ref, w_ref, lng_ref, lnb_ref, y_ref, *, alpha):
    for j in range(oa_ref.shape[0]):
        rows = slice(j * CHUNK, (j + 1) * CHUNK)
        out = (lax.dot_general(oa_ref[j], w_ref[:A_WIDTH, :], TN_DIMS, preferred_element_type=F32)
               + lax.dot_general(ob_ref[j], w_ref[A_WIDTH:, :], TN_DIMS, preferred_element_type=F32))
        z = alpha * x_ref[rows, :] + out
        mu = jnp.mean(z, axis=-1, keepdims=True)
        zc = z - mu
        var = jnp.mean(zc * zc, axis=-1, keepdims=True)
        y_ref[rows, :] = zc * lax.rsqrt(var + LN_EPS) * lng_ref[...] + lnb_ref[...]


def _output(oaT, obT, x2, w_out, ln_g, ln_b, alpha):
    C = CHUNK
    n = x2.shape[0]
    cpt = ROW_TILE // C
    const = lambda a: pl.BlockSpec(a.shape, lambda t: (0,) * a.ndim)
    return pl.pallas_call(
        functools.partial(_out_kernel, alpha=alpha),
        grid=(n // ROW_TILE,),
        in_specs=[
            pl.BlockSpec((cpt, A_WIDTH, C), lambda t: (t, 0, 0)),
            pl.BlockSpec((cpt, B_WIDTH, C), lambda t: (t, 0, 0)),
            pl.BlockSpec((ROW_TILE, D_MODEL), lambda t: (t, 0)),
            const(w_out), const(ln_g), const(ln_b),
        ],
        out_specs=pl.BlockSpec((ROW_TILE, D_MODEL), lambda t: (t, 0)),
        out_shape=jax.ShapeDtypeStruct((n, D_MODEL), F32),
        compiler_params=pltpu.CompilerParams(dimension_semantics=("parallel",),
                                             vmem_limit_bytes=VMEM_LIMIT),
        name="output",
    )(oaT, obT, x2, w_out, ln_g, ln_b)


def _prepare_weights(w_in, q_norm_g, w_uq, kv_norm_g, w_ukv):
    offs = np.concatenate([[0], np.cumsum(IN_SPLITS)])
    (wqa, wka, wva, wga, wiq, wik, wiw, wcq, wckv, wkr, wgb) = [
        w_in[:, int(offs[j]):int(offs[j + 1])] for j in range(len(IN_SPLITS))]
    d = w_in.shape[0]
    zeros = lambda c: jnp.zeros((d, c), w_in.dtype)
    widx = jnp.concatenate([wiq] + [wik] * (LANES // IDX_DIM) + [wiw, zeros(LANES - IDX_HEADS)], axis=1)
    wkpe = jnp.concatenate([zeros(B_NOPE_DIM), wkr, zeros(LANES - B_NOPE_DIM - B_ROPE_DIM)], axis=1)
    wgT = jnp.concatenate([wga, wgb], axis=1).T

    qk = B_NOPE_DIM + B_ROPE_DIM
    wuq = w_uq.reshape(Q_LORA, B_HEADS, qk)
    wuq = jnp.pad(wuq, ((0, 0), (0, 0), (0, LANES - qk))).reshape(Q_LORA, B_HEADS * LANES)
    wukv = w_ukv.reshape(KV_LORA, B_HEADS, B_NOPE_DIM + B_V_DIM)
    wukvk = jnp.pad(wukv[:, :, :B_NOPE_DIM], ((0, 0), (0, 0), (0, LANES - B_NOPE_DIM)))
    wukvk = wukvk.reshape(KV_LORA, B_HEADS * LANES)
    wukvvT = wukv[:, :, B_NOPE_DIM:].reshape(KV_LORA, B_WIDTH).T

    bf = lambda a: a.astype(BF16)
    order = (np.arange(A_HEADS)[:, None] * A_HEAD_DIM + _head64_order()[None, :]).reshape(-1)
    return [bf(wqa[:, order]), bf(wka[:, order]), widx.astype(F32), bf(wkpe), bf(wcq), bf(wckv),
            bf(wva.T), bf(wgT),
            q_norm_g.reshape(1, Q_LORA).astype(F32), kv_norm_g.reshape(1, KV_LORA).astype(F32),
            bf(wuq), bf(wukvk), bf(wukvvT)]


def kernel(x, positions, w_in, q_norm_g, w_uq, kv_norm_g, w_ukv, w_out, ln_g, ln_b):
    batch, seq, d_model = x.shape
    depth = w_in.shape[0]
    alpha = (2 * depth) ** 0.25
    assert d_model == D_MODEL and seq % ROW_TILE == 0 and ROW_TILE % CHUNK == 0
    n = batch * seq
    pos_col = positions.reshape(n, 1).astype(I32)
    x2 = x.reshape(n, d_model)
    for l in range(depth):
        weights = _prepare_weights(w_in[l], q_norm_g[l], w_uq[l], kv_norm_g[l], w_ukv[l])
        qa, ka, iq, ik, qb, kb, vaT, vbT, gT, iwT = _projection(x2, pos_col, weights)
        oaT = _dsa_attention(iq, iwT, ik, qa, ka, vaT, gT, batch, seq)
        obT = _mla_attention(qb, kb, vbT, gT, batch, seq)
        x2 = _output(oaT, obT, x2, w_out[l].astype(BF16),
                     ln_g[l].reshape(1, d_model), ln_b[l].reshape(1, d_model), alpha)
    return x2.reshape(batch, seq, d_model)
```

```python
import functools

import jax
import jax.numpy as jnp
import numpy as np
from jax import lax
from jax.experimental import pallas as pl
from jax.experimental.pallas import tpu as pltpu

F32 = jnp.float32
BF16 = jnp.bfloat16
I32 = jnp.int32
I16 = jnp.int16

D_MODEL = 1024
A_HEADS = 8
A_HEAD_DIM = 64
A_WIDTH = A_HEADS * A_HEAD_DIM
IDX_HEADS = 8
IDX_DIM = 32
TOPK_MAX = 256
B_HEADS = 8
B_NOPE_DIM = 64
B_ROPE_DIM = 32
B_V_DIM = 64
B_WIDTH = B_HEADS * B_V_DIM
Q_LORA = 256
KV_LORA = 128
ROPE_THETA = 10000.0
LN_EPS = 1e-5
RMS_EPS = 1e-6

IN_SPLITS = (A_WIDTH, A_WIDTH, A_WIDTH, A_WIDTH, IDX_HEADS * IDX_DIM, IDX_DIM, IDX_HEADS,
             Q_LORA, KV_LORA, B_ROPE_DIM, B_WIDTH)

LANES = 128
CHUNK = 256
ROW_TILE = 512
VMEM_LIMIT = 56 * 1024 * 1024
NEG = -1e30
INT_MIN = -2 ** 31
MIN16 = -2 ** 15
PACK = 16
LOG2E = 1.4426950408889634
L_ROWS = 16
LOOKAHEAD = 6

NT_DIMS = (((1,), (1,)), ((), ()))
TN_DIMS = (((0,), (0,)), ((), ()))


def _nt(a, b):
    return lax.dot_general(a, b, NT_DIMS, preferred_element_type=F32)


FREQ_BITS = 12


ROPE_HALF = IDX_DIM // 2


def _head64_order():
    i = np.arange(ROPE_HALF)
    half = A_HEAD_DIM // 2
    return np.concatenate([2 * i, half + 2 * i, 2 * i + 1, half + 2 * i + 1])


def _rope_patterns():
    lane = np.arange(LANES)
    j = lane % IDX_DIM
    freq_index = np.where(j < ROPE_HALF, 2 * j, 2 * (j - ROPE_HALF) + 1)
    rest = ROPE_THETA ** (-2.0 * freq_index.astype(np.float64) / A_HEAD_DIM)
    rows = []
    for _ in range(2):
        mant, exp = np.frexp(rest)
        piece = np.ldexp(np.floor(mant * 2 ** FREQ_BITS) / 2 ** FREQ_BITS, exp)
        rows.append(piece)
        rest = rest - piece
    rows.append(rest)
    rows.append(np.where(lane % IDX_DIM < ROPE_HALF, -1.0, 1.0))
    return jnp.asarray(np.stack(rows), F32)


def _rope_tables(pos_ref, pat_ref):
    blocks = LANES // IDX_DIM
    assert pos_ref.shape[0] == blocks * LANES
    lane = lax.broadcasted_iota(I32, (LANES, LANES), 1)
    blk = lane // IDX_DIM
    unit = lane // ROPE_HALF
    pos = pos_ref[...].astype(F32)
    p = jnp.broadcast_to(pos[(blocks - 1) * LANES:], (LANES, LANES))
    for b in range(blocks - 1):
        p = jnp.where(blk == b, pos[b * LANES:(b + 1) * LANES], p)
    a0 = p * pat_ref[0:1, :]
    dl = p * pat_ref[1:2, :] + p * pat_ref[2:3, :]
    c0, s0, cd, sd = jnp.cos(a0), jnp.sin(a0), jnp.cos(dl), jnp.sin(dl)

    def spread(x):
        rolled = [x] + [pltpu.roll(x, IDX_DIM * k, 1) for k in range(1, blocks)]
        wide, narrow = [], []
        for b in range(blocks):
            y = rolled[(blocks - 1 - b) % blocks]
            for t in range(blocks - 1):
                y = jnp.where(blk == t, rolled[(t - b) % blocks], y)
            r = pltpu.roll(y, ROPE_HALF, 1)
            wide.append(jnp.where((unit % 4 == 1) | (unit % 4 == 2), r, y))
            narrow.append(jnp.where(unit % 2 == 1, r, y))
        return jnp.concatenate(wide, axis=0), jnp.concatenate(narrow, axis=0)

    c64, c32 = spread(c0 * cd - s0 * sd)
    s64, s32 = spread(s0 * cd + c0 * sd)
    sign = pat_ref[3:4, :]
    return c64, s64 * sign, c32, s32 * sign


def _rope_group(xg, cos, sin_signed, low):
    swapped = jnp.where(low, pltpu.roll(xg, LANES - ROPE_HALF, 1), pltpu.roll(xg, ROPE_HALF, 1))
    return xg * cos + swapped * sin_signed


def _proj_kernel(x_ref, pos_ref, pat_ref,
                 wq_ref, wk_ref, widx_ref, wkpe_ref, wcq_ref, wckv_ref,
                 wvT_ref, wgT_ref, gq_ref, gkv_ref, wuq_ref, wukvk_ref, wukvvT_ref,
                 qa_ref, ka_ref, iq_ref, ik_ref, qb_ref, kb_ref,
                 vaT_ref, vbT_ref, gT_ref, iwT_ref):
    tm = x_ref.shape[0]
    xb = x_ref[...].astype(BF16)
    lane = lax.broadcasted_iota(I32, (tm, LANES), 1)
    c64, s64, c32, s32 = _rope_tables(pos_ref, pat_ref)
    low = (lane % IDX_DIM) < ROPE_HALF

    def dot(w_ref):
        return jnp.dot(xb, w_ref[...], preferred_element_type=F32)

    def rope_all(h, cos, sin, out_ref, scale=None, head_dim=None):
        for g in range(h.shape[1] // LANES):
            r = _rope_group(h[:, g * LANES:(g + 1) * LANES], cos, sin, low)
            if scale is not None:
                r = r * scale
            if head_dim is None:
                out_ref[:, g * LANES:(g + 1) * LANES] = r.astype(out_ref.dtype)
                continue
            per_group = LANES // head_dim
            for j in range(per_group):
                o = (g * per_group + j) * LANES
                out_ref[:, o:o + LANES] = jnp.where(lane // head_dim == j, r, 0.0).astype(out_ref.dtype)

    rope_all(dot(wq_ref), c64, s64, qa_ref, A_HEAD_DIM ** -0.5 * LOG2E,
             head_dim=A_HEAD_DIM)
    rope_all(dot(wk_ref), c64, s64, ka_ref)

    hidx = jnp.dot(x_ref[...], widx_ref[...], precision=lax.Precision.HIGHEST,
                   preferred_element_type=F32)
    slot = lane // IDX_DIM

    def split3(v):
        hi = v.astype(BF16).astype(F32)
        mid = (v - hi).astype(BF16).astype(F32)
        lo = (v - hi - mid).astype(BF16).astype(F32)
        return hi, mid, lo

    def idx_rope(g):
        return _rope_group(hidx[:, g * LANES:(g + 1) * LANES], c32, s32, low)

    k_hi, k_mid, k_lo = split3(idx_rope(2))
    ik_ref[:, :LANES] = jnp.where(slot == 1, k_mid, jnp.where(slot == 3, k_lo, k_hi)).astype(BF16)
    ik_ref[:, LANES:] = jnp.where(slot == 0, k_hi, jnp.where(slot == 1, k_mid, 0.0)).astype(BF16)
    slots = LANES // IDX_DIM
    for g in range(IDX_HEADS // slots):
        rolled = [[p if d == 0 else pltpu.roll(p, d * IDX_DIM, 1) for d in range(slots)]
                  for p in split3(idx_rope(g))]
        for s_ in range(slots):
            at = lambda piece, t: rolled[piece][(t - s_) % slots]
            b1 = jnp.where(slot == 0, at(0, 0), jnp.where(slot == 1, at(0, 1),
                           jnp.where(slot == 2, at(1, 2), at(0, 3))))
            b2 = jnp.where(slot == 0, at(2, 0), jnp.where(slot == 1, at(1, 1), 0.0))
            o = (g * slots + s_) * 2 * LANES
            iq_ref[:, o:o + LANES] = b1.astype(BF16)
            iq_ref[:, o + LANES:o + 2 * LANES] = b2.astype(BF16)

    def rmsnorm(c, g_ref):
        ms = jnp.mean(c * c, axis=-1, keepdims=True)
        return (c * lax.rsqrt(ms + RMS_EPS) * g_ref[...]).astype(BF16)

    cqn = rmsnorm(dot(wcq_ref), gq_ref)
    qb = jnp.dot(cqn, wuq_ref[...], preferred_element_type=F32)
    pe_lane = (lane >= B_NOPE_DIM) & (lane < B_NOPE_DIM + B_ROPE_DIM)
    b_scale = (B_NOPE_DIM + B_ROPE_DIM) ** -0.5 * LOG2E
    for h in range(B_HEADS):
        g = qb[:, h * LANES:(h + 1) * LANES]
        r = jnp.where(pe_lane, _rope_group(g, c32, s32, low), g)
        qb_ref[:, h * LANES:(h + 1) * LANES] = (r * b_scale).astype(BF16)

    ckvn = rmsnorm(dot(wckv_ref), gkv_ref)
    kpe = _rope_group(dot(wkpe_ref), c32, s32, low)
    kn = jnp.dot(ckvn, wukvk_ref[...], preferred_element_type=F32)
    for h in range(B_HEADS):
        kb_ref[:, h * LANES:(h + 1) * LANES] = (kn[:, h * LANES:(h + 1) * LANES] + kpe).astype(BF16)

    def store_t(val, out_ref):
        for j in range(tm // CHUNK):
            out_ref[j] = val[:, j * CHUNK:(j + 1) * CHUNK].astype(out_ref.dtype)

    store_t(_nt(wukvvT_ref[...], ckvn), vbT_ref)
    store_t(_nt(wvT_ref[...], xb), vaT_ref)
    gt = _nt(wgT_ref[...], xb)
    store_t(gt * jax.nn.sigmoid(gt), gT_ref)
    iw_t = jnp.transpose(hidx[:, 3 * LANES:4 * LANES])
    store_t(iw_t[:16] * (IDX_DIM ** -0.5 * IDX_HEADS ** -0.5), iwT_ref)


def _projection(x2, pos_col, weights):
    n = x2.shape[0]
    tm = ROW_TILE
    nchunks = n // CHUNK
    cpt = tm // CHUNK

    def rows(width):
        return pl.BlockSpec((tm, width), lambda t: (t, 0))

    def full(a):
        return pl.BlockSpec(a.shape, lambda t: (0,) * a.ndim)

    def tspec(feat):
        return pl.BlockSpec((cpt, feat, CHUNK), lambda t: (t, 0, 0))

    out_shape = [
        jax.ShapeDtypeStruct((n, A_HEADS * LANES), BF16),
        jax.ShapeDtypeStruct((n, A_WIDTH), BF16),
        jax.ShapeDtypeStruct((n, IDX_HEADS * 2 * LANES), BF16),
        jax.ShapeDtypeStruct((n, 2 * LANES), BF16),
        jax.ShapeDtypeStruct((n, B_HEADS * LANES), BF16),
        jax.ShapeDtypeStruct((n, B_HEADS * LANES), BF16),
        jax.ShapeDtypeStruct((nchunks, A_WIDTH, CHUNK), BF16),
        jax.ShapeDtypeStruct((nchunks, B_WIDTH, CHUNK), BF16),
        jax.ShapeDtypeStruct((nchunks, A_WIDTH + B_WIDTH, CHUNK), F32),
        jax.ShapeDtypeStruct((nchunks, 16, CHUNK), F32),
    ]
    out_specs = [rows(A_HEADS * LANES), rows(A_WIDTH), rows(IDX_HEADS * 2 * LANES), rows(2 * LANES),
                 rows(B_HEADS * LANES), rows(B_HEADS * LANES),
                 tspec(A_WIDTH), tspec(B_WIDTH), tspec(A_WIDTH + B_WIDTH), tspec(16)]
    patterns = _rope_patterns()
    in_specs = [rows(D_MODEL), rows(1), full(patterns)] + [full(w) for w in weights]
    return pl.pallas_call(
        _proj_kernel,
        grid=(n // tm,),
        in_specs=in_specs,
        out_specs=out_specs,
        out_shape=out_shape,
        compiler_params=pltpu.CompilerParams(dimension_semantics=("parallel",),
                                             vmem_limit_bytes=VMEM_LIMIT),
        name="projection",
    )(x2, pos_col, patterns, *weights)


def _softmax_step(s, smax, vt, m_ref, acc_ref, h):
    m_old = m_ref[h]
    m_new = jnp.maximum(m_old, smax)
    p = jnp.exp2(s - m_new).astype(BF16)
    vt_ones = jnp.concatenate([vt, jnp.ones((L_ROWS, vt.shape[1]), BF16)], axis=0)
    pv = jnp.dot(vt_ones, p, preferred_element_type=F32)
    acc_ref[h] = jnp.exp2(m_old - m_new) * acc_ref[h] + pv
    m_ref[h] = m_new


def _attention_pipeline(last, heads, logits, consume, s_ref, smax_ref):
    assert LOOKAHEAD < heads

    def issue(c, h):
        s = logits(c, h)
        s_ref[h] = s
        smax_ref[h] = jnp.max(s, axis=0, keepdims=True)

    def step(c, is_last):
        for h in range(heads):
            if h + LOOKAHEAD < heads:
                issue(c, h + LOOKAHEAD)
            elif not is_last:
                issue(c + 1, h + LOOKAHEAD - heads)
            consume(c, h, s_ref[h], smax_ref[h], is_last)

    for h in range(LOOKAHEAD):
        issue(0, h)

    odd = last % 2

    @pl.when(odd == 1)
    def _():
        step(0, False)

    def body(d, carry):
        c = odd + 2 * d
        step(c, False)
        step(c + 1, False)
        return carry

    lax.fori_loop(0, last // 2, body, 0)
    step(last, True)


def _softmax_init(m_ref, acc_ref):
    m_ref[...] = jnp.full(m_ref.shape, NEG, F32)
    acc_ref[...] = jnp.zeros(acc_ref.shape, F32)


def _softmax_finish(acc_ref, g_ref, o_ref, heads, dv):
    for h in range(heads):
        acc = acc_ref[h]
        o = acc[:dv] * (1.0 / acc[dv:dv + 1])
        o_ref[h * dv:(h + 1) * dv, :] = (o * g_ref[h * dv:(h + 1) * dv, :]).astype(o_ref.dtype)


def _dsa_kernel(iq_ref, iwT_ref, ik_ref, qa_ref, ka_ref, vaT_ref, g_ref, o_ref,
                hi_ref, lo_ref, bk_ref, bias_ref, m_ref, acc_ref, s_ref, smax_ref, *, k_top):
    C = CHUNK
    i = pl.program_id(1)
    nk = i + 1
    row = lax.broadcasted_iota(I32, (C, C), 0)
    col = lax.broadcasted_iota(I32, (C, C), 1)
    w = iwT_ref[...]

    def chunk_rows(c):
        return pl.ds(pl.multiple_of(c * C, C), C)

    def score_chunk(c, carry):
        ikc = ik_ref[chunk_rows(c), :]
        acc = jnp.zeros((C, C), F32)
        for h in range(IDX_HEADS):
            lg = _nt(ikc, iq_ref[:, h * 2 * LANES:(h + 1) * 2 * LANES])
            acc = acc + w[h:h + 1, :] * jnp.maximum(lg, 0.0)
        acc = acc + 0.0
        bits = lax.bitcast_convert_type(acc, I32)
        key = jnp.where(bits < 0, bits ^ jnp.int32(0x7FFFFFFF), bits)
        valid = (c * C + row) <= (i * C + col)
        key = jnp.where(valid, key, jnp.int32(INT_MIN))
        hi_ref[chunk_rows(c), :] = (key >> 16).astype(I16)
        lo_ref[chunk_rows(c), :] = (key ^ jnp.int32(0x8000)).astype(I16)
        return carry

    lax.fori_loop(0, nk // 2, lambda d, carry: score_chunk(2 * d + 1, score_chunk(2 * d, carry)), 0)

    @pl.when(nk % 2 == 1)
    def _():
        score_chunk(nk - 1, 0)

    @pl.when(nk % 2 == 1)
    def _():
        hi_ref[chunk_rows(nk), :] = jnp.full((C, C), MIN16, I16)
        lo_ref[chunk_rows(nk), :] = jnp.full((C, C), MIN16, I16)

    n_pairs = (nk + 1) // 2

    def pair_rows(d):
        return pl.ds(pl.multiple_of(d * 2 * C, 2 * C), 2 * C)

    def pack16(v):
        return jnp.broadcast_to(v, (PACK, C)).astype(I16)

    def tiles(x):
        return [x[j * PACK:(j + 1) * PACK] for j in range(x.shape[0] // PACK)]

    def tree_sum(parts):
        while len(parts) > 1:
            parts = [parts[j] + parts[j + 1] for j in range(0, len(parts), 2)]
        return parts[0]

    def count16(ref, pred):
        def body(d, acc):
            hits = [jnp.where(pred(t), jnp.int16(1), jnp.int16(0)) for t in tiles(ref[pair_rows(d), :])]
            return acc + tree_sum(hits)
        acc = lax.fori_loop(0, n_pairs, body, jnp.zeros((PACK, C), I16))
        return jnp.sum(acc.astype(I32), axis=0, keepdims=True)

    def kth_largest(ref, k):
        def bit_body(b, v):
            cand = v + jnp.left_shift(jnp.int32(1), 15 - b)
            c16 = pack16(cand)
            return jnp.where(count16(ref, lambda t: t >= c16) >= k, cand, v)
        return lax.fori_loop(0, 16, bit_body, jnp.full((1, C), MIN16, I32))

    thr_hi = kth_largest(hi_ref, k_top)
    hi16 = pack16(thr_hi)

    def bucket_pair(d, acc):
        his, los = tiles(hi_ref[pair_rows(d), :]), tiles(lo_ref[pair_rows(d), :])
        for j, (th, tl) in enumerate(zip(his, los)):
            bk_ref[pl.ds(pl.multiple_of(d * 2 * C, 2 * C) + j * PACK, PACK), :] = jnp.where(
                th == hi16, tl, jnp.int16(MIN16))
        return acc + tree_sum([jnp.where(th > hi16, jnp.int16(1), jnp.int16(0)) for th in his])

    above = lax.fori_loop(0, n_pairs, bucket_pair, jnp.zeros((PACK, C), I16))
    rank = k_top - jnp.sum(above.astype(I32), axis=0, keepdims=True)
    thr_lo = kth_largest(bk_ref, rank)
    lo16 = pack16(thr_lo)

    need = (rank - count16(bk_ref, lambda t: t > lo16)).astype(F32)
    tri = (row >= col).astype(BF16)

    def bias_chunk(c, carry):
        his, los = tiles(hi_ref[chunk_rows(c), :]), tiles(lo_ref[chunk_rows(c), :])
        eqs, gts = [], []
        for th, tl in zip(his, los):
            same_hi = th == hi16
            eqs.append(same_hi & (tl == lo16))
            gts.append((th > hi16) | (same_hi & (tl > lo16)))
        eq_b = [jnp.where(e, jnp.ones((PACK, C), BF16), jnp.zeros((PACK, C), BF16)) for e in eqs]
        prefix = jnp.dot(tri, jnp.concatenate(eq_b, axis=0), preferred_element_type=F32) + carry
        valid = (c * C + row) <= (i * C + col)
        take = tiles(jnp.where((prefix <= need) & valid, 1.0, 0.0).astype(BF16))
        bias = [jnp.where(g | (e & (t > 0)), jnp.zeros((PACK, C), BF16), jnp.full((PACK, C), NEG, BF16))
                for g, e, t in zip(gts, eqs, take)]
        bias_ref[chunk_rows(c), :] = jnp.concatenate(bias, axis=0).astype(F32)
        return carry + jnp.sum(tree_sum(eq_b).astype(F32), axis=0, keepdims=True)

    taken = lax.fori_loop(0, nk // 2, lambda d, carry: bias_chunk(2 * d + 1, bias_chunk(2 * d, carry)),
                          jnp.zeros((1, C), F32))

    @pl.when(nk % 2 == 1)
    def _():
        bias_chunk(nk - 1, taken)

    _softmax_init(m_ref, acc_ref)

    def logits(c, h):
        s = _nt(ka_ref[chunk_rows(c), (h // 2) * LANES:(h // 2 + 1) * LANES],
                qa_ref[:, h * LANES:(h + 1) * LANES])
        return s + bias_ref[chunk_rows(c), :]

    def consume(c, h, s, smax, is_last):
        _softmax_step(s, smax, vaT_ref[c, h * A_HEAD_DIM:(h + 1) * A_HEAD_DIM, :], m_ref, acc_ref, h)

    _attention_pipeline(i, A_HEADS, logits, consume, s_ref, smax_ref)
    _softmax_finish(acc_ref, g_ref, o_ref, A_HEADS, A_HEAD_DIM)


def _dsa_attention(iq, iwT, ik, qa, ka, vaT, gT, batch, seq):
    C = CHUNK
    nc = seq // C
    k_top = min(TOPK_MAX, seq // 4)
    sel_rows = (nc + nc % 2) * C
    return pl.pallas_call(
        functools.partial(_dsa_kernel, k_top=k_top),
        grid=(batch, nc),
        in_specs=[
            pl.BlockSpec((C, IDX_HEADS * 2 * LANES), lambda b, i: (b * nc + i, 0)),
            pl.BlockSpec((None, 16, C), lambda b, i: (b * nc + i, 0, 0)),
            pl.BlockSpec((seq, 2 * LANES), lambda b, i: (b, 0)),
            pl.BlockSpec((C, A_HEADS * LANES), lambda b, i: (b * nc + i, 0)),
            pl.BlockSpec((seq, A_WIDTH), lambda b, i: (b, 0)),
            pl.BlockSpec((nc, A_WIDTH, C), lambda b, i: (b, 0, 0)),
            pl.BlockSpec((None, A_WIDTH, C), lambda b, i: (b * nc + i, 0, 0)),
        ],
        out_specs=pl.BlockSpec((None, A_WIDTH, C), lambda b, i: (b * nc + i, 0, 0)),
        out_shape=jax.ShapeDtypeStruct((batch * nc, A_WIDTH, C), BF16),
        scratch_shapes=[
            pltpu.VMEM((sel_rows, C), I16),
            pltpu.VMEM((sel_rows, C), I16),
            pltpu.VMEM((sel_rows, C), I16),
            pltpu.VMEM((seq, C), F32),
            pltpu.VMEM((A_HEADS, 1, C), F32),
            pltpu.VMEM((A_HEADS, A_HEAD_DIM + L_ROWS, C), F32),
            pltpu.VMEM((A_HEADS, C, C), F32),
            pltpu.VMEM((A_HEADS, 1, C), F32),
        ],
        compiler_params=pltpu.CompilerParams(dimension_semantics=("parallel", "arbitrary"),
                                             vmem_limit_bytes=VMEM_LIMIT),
        name="dsa_attention",
    )(iq, iwT, ik, qa, ka, vaT, gT)


def _mla_kernel(qb_ref, kb_ref, vbT_ref, g_ref, o_ref, m_ref, acc_ref, s_ref, smax_ref):
    C = CHUNK
    i = pl.program_id(1)
    row = lax.broadcasted_iota(I32, (C, C), 0)
    col = lax.broadcasted_iota(I32, (C, C), 1)
    causal = row <= col
    _softmax_init(m_ref, acc_ref)

    def logits(c, h):
        kc = kb_ref[pl.ds(pl.multiple_of(c * C, C), C), h * LANES:(h + 1) * LANES]
        return _nt(kc, qb_ref[:, h * LANES:(h + 1) * LANES])

    def consume(c, h, s, smax, is_last):
        if is_last:
            s = jnp.where(causal, s, NEG)
            smax = jnp.max(s, axis=0, keepdims=True)
        _softmax_step(s, smax, vbT_ref[c, h * B_V_DIM:(h + 1) * B_V_DIM, :], m_ref, acc_ref, h)

    _attention_pipeline(i, B_HEADS, logits, consume, s_ref, smax_ref)
    _softmax_finish(acc_ref, g_ref, o_ref, B_HEADS, B_V_DIM)


def _mla_attention(qb, kb, vbT, gT, batch, seq):
    C = CHUNK
    nc = seq // C
    return pl.pallas_call(
        _mla_kernel,
        grid=(batch, nc),
        in_specs=[
            pl.BlockSpec((C, B_HEADS * LANES), lambda b, i: (b * nc + i, 0)),
            pl.BlockSpec((seq, B_HEADS * LANES), lambda b, i: (b, 0)),
            pl.BlockSpec((nc, B_WIDTH, C), lambda b, i: (b, 0, 0)),
            pl.BlockSpec((None, B_WIDTH, C), lambda b, i: (b * nc + i, 1, 0)),
        ],
        out_specs=pl.BlockSpec((None, B_WIDTH, C), lambda b, i: (b * nc + i, 0, 0)),
        out_shape=jax.ShapeDtypeStruct((batch * nc, B_WIDTH, C), BF16),
        scratch_shapes=[
            pltpu.VMEM((B_HEADS, 1, C), F32),
            pltpu.VMEM((B_HEADS, B_V_DIM + L_ROWS, C), F32),
            pltpu.VMEM((B_HEADS, C, C), F32),
            pltpu.VMEM((B_HEADS, 1, C), F32),
        ],
        compiler_params=pltpu.CompilerParams(dimension_semantics=("parallel", "arbitrary"),
                                             vmem_limit_bytes=VMEM_LIMIT),
        name="mla_attention",
    )(qb, kb, vbT, gT)


def _out_kernel(oa_ref, ob_ref, x_ref, w_ref, lng_ref, lnb_ref, y_ref, *, alpha):
    for j in range(oa_ref.shape[0]):
        rows = slice(j * CHUNK, (j + 1) * CHUNK)
        out = (lax.dot_general(oa_ref[j], w_ref[:A_WIDTH, :], TN_DIMS, preferred_element_type=F32)
               + lax.dot_general(ob_ref[j], w_ref[A_WIDTH:, :], TN_DIMS, preferred_element_type=F32))
        z = alpha * x_ref[rows, :] + out
        mu = jnp.mean(z, axis=-1, keepdims=True)
        zc = z - mu
        var = jnp.mean(zc * zc, axis=-1, keepdims=True)
        y_ref[rows, :] = zc * lax.rsqrt(var + LN_EPS) * lng_ref[...] + lnb_ref[...]


def _output(oaT, obT, x2, w_out, ln_g, ln_b, alpha):
    C = CHUNK
    n = x2.shape[0]
    cpt = ROW_TILE // C
    const = lambda a: pl.BlockSpec(a.shape, lambda t: (0,) * a.ndim)
    return pl.pallas_call(
        functools.partial(_out_kernel, alpha=alpha),
        grid=(n // ROW_TILE,),
        in_specs=[
            pl.BlockSpec((cpt, A_WIDTH, C), lambda t: (t, 0, 0)),
            pl.BlockSpec((cpt, B_WIDTH, C), lambda t: (t, 0, 0)),
            pl.BlockSpec((ROW_TILE, D_MODEL), lambda t: (t, 0)),
            const(w_out), const(ln_g), const(ln_b),
        ],
        out_specs=pl.BlockSpec((ROW_TILE, D_MODEL), lambda t: (t, 0)),
        out_shape=jax.ShapeDtypeStruct((n, D_MODEL), F32),
        compiler_params=pltpu.CompilerParams(dimension_semantics=("parallel",),
                                             vmem_limit_bytes=VMEM_LIMIT),
        name="output",
    )(oaT, obT, x2, w_out, ln_g, ln_b)


def _prepare_weights(w_in, q_norm_g, w_uq, kv_norm_g, w_ukv):
    offs = np.concatenate([[0], np.cumsum(IN_SPLITS)])
    (wqa, wka, wva, wga, wiq, wik, wiw, wcq, wckv, wkr, wgb) = [
        w_in[:, int(offs[j]):int(offs[j + 1])] for j in range(len(IN_SPLITS))]
    d = w_in.shape[0]
    zeros = lambda c: jnp.zeros((d, c), w_in.dtype)
    widx = jnp.concatenate([wiq] + [wik] * (LANES // IDX_DIM) + [wiw, zeros(LANES - IDX_HEADS)], axis=1)
    wkpe = jnp.concatenate([zeros(B_NOPE_DIM), wkr, zeros(LANES - B_NOPE_DIM - B_ROPE_DIM)], axis=1)
    wgT = jnp.concatenate([wga, wgb], axis=1).T

    qk = B_NOPE_DIM + B_ROPE_DIM
    wuq = w_uq.reshape(Q_LORA, B_HEADS, qk)
    wuq = jnp.pad(wuq, ((0, 0), (0, 0), (0, LANES - qk))).reshape(Q_LORA, B_HEADS * LANES)
    wukv = w_ukv.reshape(KV_LORA, B_HEADS, B_NOPE_DIM + B_V_DIM)
    wukvk = jnp.pad(wukv[:, :, :B_NOPE_DIM], ((0, 0), (0, 0), (0, LANES - B_NOPE_DIM)))
    wukvk = wukvk.reshape(KV_LORA, B_HEADS * LANES)
    wukvvT = wukv[:, :, B_NOPE_DIM:].reshape(KV_LORA, B_WIDTH).T

    bf = lambda a: a.astype(BF16)
    order = (np.arange(A_HEADS)[:, None] * A_HEAD_DIM + _head64_order()[None, :]).reshape(-1)
    return [bf(wqa[:, order]), bf(wka[:, order]), widx.astype(F32), bf(wkpe), bf(wcq), bf(wckv),
            bf(wva.T), bf(wgT),
            q_norm_g.reshape(1, Q_LORA).astype(F32), kv_norm_g.reshape(1, KV_LORA).astype(F32),
            bf(wuq), bf(wukvk), bf(wukvvT)]


def kernel(x, positions, w_in, q_norm_g, w_uq, kv_norm_g, w_ukv, w_out, ln_g, ln_b):
    batch, seq, d_model = x.shape
    depth = w_in.shape[0]
    alpha = (2 * depth) ** 0.25
    assert d_model == D_MODEL and seq % ROW_TILE == 0 and ROW_TILE % CHUNK == 0
    n = batch * seq
    pos_col = positions.reshape(n, 1).astype(I32)
    x2 = x.reshape(n, d_model)
    for l in range(depth):
        weights = _prepare_weights(w_in[l], q_norm_g[l], w_uq[l], kv_norm_g[l], w_ukv[l])
        qa, ka, iq, ik, qb, kb, vaT, vbT, gT, iwT = _projection(x2, pos_col, weights)
        oaT = _dsa_attention(iq, iwT, ik, qa, ka, vaT, gT, batch, seq)
        obT = _mla_attention(qb, kb, vbT, gT, batch, seq)
        x2 = _output(oaT, obT, x2, w_out[l].astype(BF16),
                     ln_g[l].reshape(1, d_model), ln_b[l].reshape(1, d_model), alpha)
    return x2.reshape(batch, seq, d_model)
```

```python
import functools

import jax
import jax.numpy as jnp
import numpy as np
from jax import lax
from jax.experimental import pallas as pl
from jax.experimental.pallas import tpu as pltpu

F32 = jnp.float32
BF16 = jnp.bfloat16
I32 = jnp.int32
I16 = jnp.int16

D_MODEL = 1024
A_HEADS = 8
A_HEAD_DIM = 64
A_WIDTH = A_HEADS * A_HEAD_DIM
IDX_HEADS = 8
IDX_DIM = 32
TOPK_MAX = 256
B_HEADS = 8
B_NOPE_DIM = 64
B_ROPE_DIM = 32
B_V_DIM = 64
B_WIDTH = B_HEADS * B_V_DIM
Q_LORA = 256
KV_LORA = 128
ROPE_THETA = 10000.0
LN_EPS = 1e-5
RMS_EPS = 1e-6

IN_SPLITS = (A_WIDTH, A_WIDTH, A_WIDTH, A_WIDTH, IDX_HEADS * IDX_DIM, IDX_DIM, IDX_HEADS,
             Q_LORA, KV_LORA, B_ROPE_DIM, B_WIDTH)

LANES = 128
CHUNK = 256
ROW_TILE = 512
VMEM_LIMIT = 56 * 1024 * 1024
NEG = -1e30
INT_MIN = -2 ** 31
MIN16 = -2 ** 15
PACK = 16
LOG2E = 1.4426950408889634
L_ROWS = 16
LOOKAHEAD = 6

NT_DIMS = (((1,), (1,)), ((), ()))
TN_DIMS = (((0,), (0,)), ((), ()))


def _nt(a, b):
    return lax.dot_general(a, b, NT_DIMS, preferred_element_type=F32)


FREQ_BITS = 12


ROPE_HALF = IDX_DIM // 2


def _head64_order():
    i = np.arange(ROPE_HALF)
    half = A_HEAD_DIM // 2
    return np.concatenate([2 * i, half + 2 * i, 2 * i + 1, half + 2 * i + 1])


def _rope_patterns():
    lane = np.arange(LANES)
    j = lane % IDX_DIM
    freq_index = np.where(j < ROPE_HALF, 2 * j, 2 * (j - ROPE_HALF) + 1)
    rest = ROPE_THETA ** (-2.0 * freq_index.astype(np.float64) / A_HEAD_DIM)
    rows = []
    for _ in range(2):
        mant, exp = np.frexp(rest)
        piece = np.ldexp(np.floor(mant * 2 ** FREQ_BITS) / 2 ** FREQ_BITS, exp)
        rows.append(piece)
        rest = rest - piece
    rows.append(rest)
    rows.append(np.where(lane % IDX_DIM < ROPE_HALF, -1.0, 1.0))
    return jnp.asarray(np.stack(rows), F32)


def _rope_tables(pos_ref, pat_ref):
    blocks = LANES // IDX_DIM
    assert pos_ref.shape[0] == blocks * LANES
    lane = lax.broadcasted_iota(I32, (LANES, LANES), 1)
    blk = lane // IDX_DIM
    unit = lane // ROPE_HALF
    pos = pos_ref[...].astype(F32)
    p = jnp.broadcast_to(pos[(blocks - 1) * LANES:], (LANES, LANES))
    for b in range(blocks - 1):
        p = jnp.where(blk == b, pos[b * LANES:(b + 1) * LANES], p)
    a0 = p * pat_ref[0:1, :]
    dl = p * pat_ref[1:2, :] + p * pat_ref[2:3, :]
    c0, s0, cd, sd = jnp.cos(a0), jnp.sin(a0), jnp.cos(dl), jnp.sin(dl)

    def spread(x):
        rolled = [x] + [pltpu.roll(x, IDX_DIM * k, 1) for k in range(1, blocks)]
        wide, narrow = [], []
        for b in range(blocks):
            y = rolled[(blocks - 1 - b) % blocks]
            for t in range(blocks - 1):
                y = jnp.where(blk == t, rolled[(t - b) % blocks], y)
            r = pltpu.roll(y, ROPE_HALF, 1)
            wide.append(jnp.where((unit % 4 == 1) | (unit % 4 == 2), r, y))
            narrow.append(jnp.where(unit % 2 == 1, r, y))
        return jnp.concatenate(wide, axis=0), jnp.concatenate(narrow, axis=0)

    c64, c32 = spread(c0 * cd - s0 * sd)
    s64, s32 = spread(s0 * cd + c0 * sd)
    sign = pat_ref[3:4, :]
    return c64, s64 * sign, c32, s32 * sign


def _rope_group(xg, cos, sin_signed, low):
    swapped = jnp.where(low, pltpu.roll(xg, LANES - ROPE_HALF, 1), pltpu.roll(xg, ROPE_HALF, 1))
    return xg * cos + swapped * sin_signed


def _proj_kernel(x_ref, pos_ref, pat_ref,
                 wq_ref, wk_ref, widx_ref, wkpe_ref, wcq_ref, wckv_ref,
                 wvT_ref, wgT_ref, gq_ref, gkv_ref, wuq_ref, wukvk_ref, wukvvT_ref,
                 qa_ref, ka_ref, iq_ref, ik_ref, qb_ref, kb_ref,
                 vaT_ref, vbT_ref, gT_ref, iwT_ref):
    tm = x_ref.shape[0]
    xb = x_ref[...].astype(BF16)
    lane = lax.broadcasted_iota(I32, (tm, LANES), 1)
    c64, s64, c32, s32 = _rope_tables(pos_ref, pat_ref)
    low = (lane % IDX_DIM) < ROPE_HALF

    def dot(w_ref):
        return jnp.dot(xb, w_ref[...], preferred_element_type=F32)

    def rope_all(h, cos, sin, out_ref, scale=None, head_dim=None):
        for g in range(h.shape[1] // LANES):
            r = _rope_group(h[:, g * LANES:(g + 1) * LANES], cos, sin, low)
            if scale is not None:
                r = r * scale
            if head_dim is None:
                out_ref[:, g * LANES:(g + 1) * LANES] = r.astype(out_ref.dtype)
                continue
            per_group = LANES // head_dim
            for j in range(per_group):
                o = (g * per_group + j) * LANES
                out_ref[:, o:o + LANES] = jnp.where(lane // head_dim == j, r, 0.0).astype(out_ref.dtype)

    rope_all(dot(wq_ref), c64, s64, qa_ref, A_HEAD_DIM ** -0.5 * LOG2E,
             head_dim=A_HEAD_DIM)
    rope_all(dot(wk_ref), c64, s64, ka_ref)

    hidx = jnp.dot(x_ref[...], widx_ref[...], precision=lax.Precision.HIGHEST,
                   preferred_element_type=F32)
    slot = lane // IDX_DIM

    def split3(v):
        hi = v.astype(BF16).astype(F32)
        mid = (v - hi).astype(BF16).astype(F32)
        lo = (v - hi - mid).astype(BF16).astype(F32)
        return hi, mid, lo

    def idx_rope(g):
        return _rope_group(hidx[:, g * LANES:(g + 1) * LANES], c32, s32, low)

    k_hi, k_mid, k_lo = split3(idx_rope(2))
    ik_ref[:, :LANES] = jnp.where(slot == 1, k_mid, jnp.where(slot == 3, k_lo, k_hi)).astype(BF16)
    ik_ref[:, LANES:] = jnp.where(slot == 0, k_hi, jnp.where(slot == 1, k_mid, 0.0)).astype(BF16)
    slots = LANES // IDX_DIM
    for g in range(IDX_HEADS // slots):
        rolled = [[p if d == 0 else pltpu.roll(p, d * IDX_DIM, 1) for d in range(slots)]
                  for p in split3(idx_rope(g))]
        for s_ in range(slots):
            at = lambda piece, t: rolled[piece][(t - s_) % slots]
            b1 = jnp.where(slot == 0, at(0, 0), jnp.where(slot == 1, at(0, 1),
                           jnp.where(slot == 2, at(1, 2), at(0, 3))))
            b2 = jnp.where(slot == 0, at(2, 0), jnp.where(slot == 1, at(1, 1), 0.0))
            o = (g * slots + s_) * 2 * LANES
            iq_ref[:, o:o + LANES] = b1.astype(BF16)
            iq_ref[:, o + LANES:o + 2 * LANES] = b2.astype(BF16)

    def rmsnorm(c, g_ref):
        ms = jnp.mean(c * c, axis=-1, keepdims=True)
        return (c * lax.rsqrt(ms + RMS_EPS) * g_ref[...]).astype(BF16)

    cqn = rmsnorm(dot(wcq_ref), gq_ref)
    qb = jnp.dot(cqn, wuq_ref[...], preferred_element_type=F32)
    pe_lane = (lane >= B_NOPE_DIM) & (lane < B_NOPE_DIM + B_ROPE_DIM)
    b_scale = (B_NOPE_DIM + B_ROPE_DIM) ** -0.5 * LOG2E
    for h in range(B_HEADS):
        g = qb[:, h * LANES:(h + 1) * LANES]
        r = jnp.where(pe_lane, _rope_group(g, c32, s32, low), g)
        qb_ref[:, h * LANES:(h + 1) * LANES] = (r * b_scale).astype(BF16)

    ckvn = rmsnorm(dot(wckv_ref), gkv_ref)
    kpe = _rope_group(dot(wkpe_ref), c32, s32, low)
    kn = jnp.dot(ckvn, wukvk_ref[...], preferred_element_type=F32)
    for h in range(B_HEADS):
        kb_ref[:, h * LANES:(h + 1) * LANES] = (kn[:, h * LANES:(h + 1) * LANES] + kpe).astype(BF16)

    def store_t(val, out_ref):
        for j in range(tm // CHUNK):
            out_ref[j] = val[:, j * CHUNK:(j + 1) * CHUNK].astype(out_ref.dtype)

    store_t(_nt(wukvvT_ref[...], ckvn), vbT_ref)
    store_t(_nt(wvT_ref[...], xb), vaT_ref)
    gt = _nt(wgT_ref[...], xb)
    store_t(gt * jax.nn.sigmoid(gt), gT_ref)
    iw_t = jnp.transpose(hidx[:, 3 * LANES:4 * LANES])
    store_t(iw_t[:16] * (IDX_DIM ** -0.5 * IDX_HEADS ** -0.5), iwT_ref)


def _projection(x2, pos_col, weights):
    n = x2.shape[0]
    tm = ROW_TILE
    nchunks = n // CHUNK
    cpt = tm // CHUNK

    def rows(width):
        return pl.BlockSpec((tm, width), lambda t: (t, 0))

    def full(a):
        return pl.BlockSpec(a.shape, lambda t: (0,) * a.ndim)

    def tspec(feat):
        return pl.BlockSpec((cpt, feat, CHUNK), lambda t: (t, 0, 0))

    out_shape = [
        jax.ShapeDtypeStruct((n, A_HEADS * LANES), BF16),
        jax.ShapeDtypeStruct((n, A_WIDTH), BF16),
        jax.ShapeDtypeStruct((n, IDX_HEADS * 2 * LANES), BF16),
        jax.ShapeDtypeStruct((n, 2 * LANES), BF16),
        jax.ShapeDtypeStruct((n, B_HEADS * LANES), BF16),
        jax.ShapeDtypeStruct((n, B_HEADS * LANES), BF16),
        jax.ShapeDtypeStruct((nchunks, A_WIDTH, CHUNK), BF16),
        jax.ShapeDtypeStruct((nchunks, B_WIDTH, CHUNK), BF16),
        jax.ShapeDtypeStruct((nchunks, A_WIDTH + B_WIDTH, CHUNK), F32),
        jax.ShapeDtypeStruct((nchunks, 16, CHUNK), F32),
    ]
    out_specs = [rows(A_HEADS * LANES), rows(A_WIDTH), rows(IDX_HEADS * 2 * LANES), rows(2 * LANES),
                 rows(B_HEADS * LANES), rows(B_HEADS * LANES),
                 tspec(A_WIDTH), tspec(B_WIDTH), tspec(A_WIDTH + B_WIDTH), tspec(16)]
    patterns = _rope_patterns()
    in_specs = [rows(D_MODEL), rows(1), full(patterns)] + [full(w) for w in weights]
    return pl.pallas_call(
        _proj_kernel,
        grid=(n // tm,),
        in_specs=in_specs,
        out_specs=out_specs,
        out_shape=out_shape,
        compiler_params=pltpu.CompilerParams(dimension_semantics=("parallel",),
                                             vmem_limit_bytes=VMEM_LIMIT),
        name="projection",
    )(x2, pos_col, patterns, *weights)


def _softmax_step(s, smax, vt, m_ref, acc_ref, h):
    m_old = m_ref[h]
    m_new = jnp.maximum(m_old, smax)
    p = jnp.exp2(s - m_new).astype(BF16)
    vt_ones = jnp.concatenate([vt, jnp.ones((L_ROWS, vt.shape[1]), BF16)], axis=0)
    pv = jnp.dot(vt_ones, p, preferred_element_type=F32)
    acc_ref[h] = jnp.exp2(m_old - m_new) * acc_ref[h] + pv
    m_ref[h] = m_new


def _attention_pipeline(last, heads, logits, consume, s_ref, smax_ref):
    assert LOOKAHEAD < heads

    def issue(c, h):
        s = logits(c, h)
        s_ref[h] = s
        smax_ref[h] = jnp.max(s, axis=0, keepdims=True)

    def step(c, is_last):
        for h in range(heads):
            if h + LOOKAHEAD < heads:
                issue(c, h + LOOKAHEAD)
            elif not is_last:
                issue(c + 1, h + LOOKAHEAD - heads)
            consume(c, h, s_ref[h], smax_ref[h], is_last)

    for h in range(LOOKAHEAD):
        issue(0, h)

    odd = last % 2

    @pl.when(odd == 1)
    def _():
        step(0, False)

    def body(d, carry):
        c = odd + 2 * d
        step(c, False)
        step(c + 1, False)
        return carry

    lax.fori_loop(0, last // 2, body, 0)
    step(last, True)


def _softmax_init(m_ref, acc_ref):
    m_ref[...] = jnp.full(m_ref.shape, NEG, F32)
    acc_ref[...] = jnp.zeros(acc_ref.shape, F32)


def _softmax_finish(acc_ref, g_ref, o_ref, heads, dv):
    for h in range(heads):
        acc = acc_ref[h]
        o = acc[:dv] * (1.0 / acc[dv:dv + 1])
        o_ref[h * dv:(h + 1) * dv, :] = (o * g_ref[h * dv:(h + 1) * dv, :]).astype(o_ref.dtype)


def _dsa_kernel(iq_ref, iwT_ref, ik_ref, qa_ref, ka_ref, vaT_ref, g_ref, o_ref,
                hi_ref, lo_ref, bk_ref, bias_ref, m_ref, acc_ref, s_ref, smax_ref, *, k_top):
    C = CHUNK
    i = pl.program_id(1)
    nk = i + 1
    row = lax.broadcasted_iota(I32, (C, C), 0)
    col = lax.broadcasted_iota(I32, (C, C), 1)
    w = iwT_ref[...]

    def chunk_rows(c):
        return pl.ds(pl.multiple_of(c * C, C), C)

    def score_chunk(c, carry):
        ikc = ik_ref[chunk_rows(c), :]
        acc = jnp.zeros((C, C), F32)
        for h in range(IDX_HEADS):
            lg = _nt(ikc, iq_ref[:, h * 2 * LANES:(h + 1) * 2 * LANES])
            acc = acc + w[h:h + 1, :] * jnp.maximum(lg, 0.0)
        acc = acc + 0.0
        bits = lax.bitcast_convert_type(acc, I32)
        key = jnp.where(bits < 0, bits ^ jnp.int32(0x7FFFFFFF), bits)
        valid = (c * C + row) <= (i * C + col)
        key = jnp.where(valid, key, jnp.int32(INT_MIN))
        hi_ref[chunk_rows(c), :] = (key >> 16).astype(I16)
        lo_ref[chunk_rows(c), :] = (key ^ jnp.int32(0x8000)).astype(I16)
        return carry

    lax.fori_loop(0, nk // 2, lambda d, carry: score_chunk(2 * d + 1, score_chunk(2 * d, carry)), 0)

    @pl.when(nk % 2 == 1)
    def _():
        score_chunk(nk - 1, 0)

    @pl.when(nk % 2 == 1)
    def _():
        hi_ref[chunk_rows(nk), :] = jnp.full((C, C), MIN16, I16)
        lo_ref[chunk_rows(nk), :] = jnp.full((C, C), MIN16, I16)

    n_pairs = (nk + 1) // 2

    def pair_rows(d):
        return pl.ds(pl.multiple_of(d * 2 * C, 2 * C), 2 * C)

    def pack16(v):
        return jnp.broadcast_to(v, (PACK, C)).astype(I16)

    def tiles(x):
        return [x[j * PACK:(j + 1) * PACK] for j in range(x.shape[0] // PACK)]

    def tree_sum(parts):
        while len(parts) > 1:
            parts = [parts[j] + parts[j + 1] for j in range(0, len(parts), 2)]
        return parts[0]

    def count16(ref, pred):
        def body(d, acc):
            hits = [jnp.where(pred(t), jnp.int16(1), jnp.int16(0)) for t in tiles(ref[pair_rows(d), :])]
            return acc + tree_sum(hits)
        acc = lax.fori_loop(0, n_pairs, body, jnp.zeros((PACK, C), I16))
        return jnp.sum(acc.astype(I32), axis=0, keepdims=True)

    def kth_largest(ref, k):
        def bit_body(b, v):
            cand = v + jnp.left_shift(jnp.int32(1), 15 - b)
            c16 = pack16(cand)
            return jnp.where(count16(ref, lambda t: t >= c16) >= k, cand, v)
        return lax.fori_loop(0, 16, bit_body, jnp.full((1, C), MIN16, I32))

    thr_hi = kth_largest(hi_ref, k_top)
    hi16 = pack16(thr_hi)

    def bucket_pair(d, acc):
        his, los = tiles(hi_ref[pair_rows(d), :]), tiles(lo_ref[pair_rows(d), :])
        for j, (th, tl) in enumerate(zip(his, los)):
            bk_ref[pl.ds(pl.multiple_of(d * 2 * C, 2 * C) + j * PACK, PACK), :] = jnp.where(
                th == hi16, tl, jnp.int16(MIN16))
        return acc + tree_sum([jnp.where(th > hi16, jnp.int16(1), jnp.int16(0)) for th in his])

    above = lax.fori_loop(0, n_pairs, bucket_pair, jnp.zeros((PACK, C), I16))
    rank = k_top - jnp.sum(above.astype(I32), axis=0, keepdims=True)
    thr_lo = kth_largest(bk_ref, rank)
    lo16 = pack16(thr_lo)

    need = (rank - count16(bk_ref, lambda t: t > lo16)).astype(F32)
    tri = (row >= col).astype(BF16)

    def bias_chunk(c, carry):
        his, los = tiles(hi_ref[chunk_rows(c), :]), tiles(lo_ref[chunk_rows(c), :])
        eqs, gts = [], []
        for th, tl in zip(his, los):
            same_hi = th == hi16
            eqs.append(same_hi & (tl == lo16))
            gts.append((th > hi16) | (same_hi & (tl > lo16)))
        eq_b = [jnp.where(e, jnp.ones((PACK, C), BF16), jnp.zeros((PACK, C), BF16)) for e in eqs]
        prefix = jnp.dot(tri, jnp.concatenate(eq_b, axis=0), preferred_element_type=F32) + carry
        valid = (c * C + row) <= (i * C + col)
        take = tiles(jnp.where((prefix <= need) & valid, 1.0, 0.0).astype(BF16))
        bias = [jnp.where(g | (e & (t > 0)), jnp.zeros((PACK, C), BF16), jnp.full((PACK, C), NEG, BF16))
                for g, e, t in zip(gts, eqs, take)]
        bias_ref[chunk_rows(c), :] = jnp.concatenate(bias, axis=0).astype(F32)
        return carry + jnp.sum(tree_sum(eq_b).astype(F32), axis=0, keepdims=True)

    taken = lax.fori_loop(0, nk // 2, lambda d, carry: bias_chunk(2 * d + 1, bias_chunk(2 * d, carry)),
                          jnp.zeros((1, C), F32))

    @pl.when(nk % 2 == 1)
    def _():
        bias_chunk(nk - 1, taken)

    _softmax_init(m_ref, acc_ref)

    def logits(c, h):
        s = _nt(ka_ref[chunk_rows(c), (h // 2) * LANES:(h // 2 + 1) * LANES],
                qa_ref[:, h * LANES:(h + 1) * LANES])
        return s + bias_ref[chunk_rows(c), :]

    def consume(c, h, s, smax, is_last):
        _softmax_step(s, smax, vaT_ref[c, h * A_HEAD_DIM:(h + 1) * A_HEAD_DIM, :], m_ref, acc_ref, h)

    _attention_pipeline(i, A_HEADS, logits, consume, s_ref, smax_ref)
    _softmax_finish(acc_ref, g_ref, o_ref, A_HEADS, A_HEAD_DIM)


def _dsa_attention(iq, iwT, ik, qa, ka, vaT, gT, batch, seq):
    C = CHUNK
    nc = seq // C
    k_top = min(TOPK_MAX, seq // 4)
    sel_rows = (nc + nc % 2) * C
    return pl.pallas_call(
        functools.partial(_dsa_kernel, k_top=k_top),
        grid=(batch, nc),
        in_specs=[
            pl.BlockSpec((C, IDX_HEADS * 2 * LANES), lambda b, i: (b * nc + i, 0)),
            pl.BlockSpec((None, 16, C), lambda b, i: (b * nc + i, 0, 0)),
            pl.BlockSpec((seq, 2 * LANES), lambda b, i: (b, 0)),
            pl.BlockSpec((C, A_HEADS * LANES), lambda b, i: (b * nc + i, 0)),
            pl.BlockSpec((seq, A_WIDTH), lambda b, i: (b, 0)),
            pl.BlockSpec((nc, A_WIDTH, C), lambda b, i: (b, 0, 0)),
            pl.BlockSpec((None, A_WIDTH, C), lambda b, i: (b * nc + i, 0, 0)),
        ],
        out_specs=pl.BlockSpec((None, A_WIDTH, C), lambda b, i: (b * nc + i, 0, 0)),
        out_shape=jax.ShapeDtypeStruct((batch * nc, A_WIDTH, C), BF16),
        scratch_shapes=[
            pltpu.VMEM((sel_rows, C), I16),
            pltpu.VMEM((sel_rows, C), I16),
            pltpu.VMEM((sel_rows, C), I16),
            pltpu.VMEM((seq, C), F32),
            pltpu.VMEM((A_HEADS, 1, C), F32),
            pltpu.VMEM((A_HEADS, A_HEAD_DIM + L_ROWS, C), F32),
            pltpu.VMEM((A_HEADS, C, C), F32),
            pltpu.VMEM((A_HEADS, 1, C), F32),
        ],
        compiler_params=pltpu.CompilerParams(dimension_semantics=("parallel", "arbitrary"),
                                             vmem_limit_bytes=VMEM_LIMIT),
        name="dsa_attention",
    )(iq, iwT, ik, qa, ka, vaT, gT)


def _mla_kernel(qb_ref, kb_ref, vbT_ref, g_ref, o_ref, m_ref, acc_ref, s_ref, smax_ref):
    C = CHUNK
    i = pl.program_id(1)
    row = lax.broadcasted_iota(I32, (C, C), 0)
    col = lax.broadcasted_iota(I32, (C, C), 1)
    causal = row <= col
    _softmax_init(m_ref, acc_ref)

    def logits(c, h):
        kc = kb_ref[pl.ds(pl.multiple_of(c * C, C), C), h * LANES:(h + 1) * LANES]
        return _nt(kc, qb_ref[:, h * LANES:(h + 1) * LANES])

    def consume(c, h, s, smax, is_last):
        if is_last:
            s = jnp.where(causal, s, NEG)
            smax = jnp.max(s, axis=0, keepdims=True)
        _softmax_step(s, smax, vbT_ref[c, h * B_V_DIM:(h + 1) * B_V_DIM, :], m_ref, acc_ref, h)

    _attention_pipeline(i, B_HEADS, logits, consume, s_ref, smax_ref)
    _softmax_finish(acc_ref, g_ref, o_ref, B_HEADS, B_V_DIM)


def _mla_attention(qb, kb, vbT, gT, batch, seq):
    C = CHUNK
    nc = seq // C
    return pl.pallas_call(
        _mla_kernel,
        grid=(batch, nc),
        in_specs=[
            pl.BlockSpec((C, B_HEADS * LANES), lambda b, i: (b * nc + i, 0)),
            pl.BlockSpec((seq, B_HEADS * LANES), lambda b, i: (b, 0)),
            pl.BlockSpec((nc, B_WIDTH, C), lambda b, i: (b, 0, 0)),
            pl.BlockSpec((None, B_WIDTH, C), lambda b, i: (b * nc + i, 1, 0)),
        ],
        out_specs=pl.BlockSpec((None, B_WIDTH, C), lambda b, i: (b * nc + i, 0, 0)),
        out_shape=jax.ShapeDtypeStruct((batch * nc, B_WIDTH, C), BF16),
        scratch_shapes=[
            pltpu.VMEM((B_HEADS, 1, C), F32),
            pltpu.VMEM((B_HEADS, B_V_DIM + L_ROWS, C), F32),
            pltpu.VMEM((B_HEADS, C, C), F32),
            pltpu.VMEM((B_HEADS, 1, C), F32),
        ],
        compiler_params=pltpu.CompilerParams(dimension_semantics=("parallel", "arbitrary"),
                                             vmem_limit_bytes=VMEM_LIMIT),
        name="mla_attention",
    )(qb, kb, vbT, gT)


def _out_kernel(oa_ref, ob_ref, x_ref, w_ref, lng_ref, lnb_ref, y_ref, *, alpha):
    for j in range(oa_ref.shape[0]):
        rows = slice(j * CHUNK, (j + 1) * CHUNK)
        out = (lax.dot_general(oa_ref[j], w_ref[:A_WIDTH, :], TN_DIMS, preferred_element_type=F32)
               + lax.dot_general(ob_ref[j], w_ref[A_WIDTH:, :], TN_DIMS, preferred_element_type=F32))
        z = alpha * x_ref[rows, :] + out
        mu = jnp.mean(z, axis=-1, keepdims=True)
        zc = z - mu
        var = jnp.mean(zc * zc, axis=-1, keepdims=True)
        y_ref[rows, :] = zc * lax.rsqrt(var + LN_EPS) * lng_ref[...] + lnb_ref[...]


def _output(oaT, obT, x2, w_out, ln_g, ln_b, alpha):
    C = CHUNK
    n = x2.shape[0]
    tile = 2 * ROW_TILE
    assert n % tile == 0
    cpt = tile // C
    const = lambda a: pl.BlockSpec(a.shape, lambda t: (0,) * a.ndim)
    return pl.pallas_call(
        functools.partial(_out_kernel, alpha=alpha),
        grid=(n // tile,),
        in_specs=[
            pl.BlockSpec((cpt, A_WIDTH, C), lambda t: (t, 0, 0)),
            pl.BlockSpec((cpt, B_WIDTH, C), lambda t: (t, 0, 0)),
            pl.BlockSpec((tile, D_MODEL), lambda t: (t, 0)),
            const(w_out), const(ln_g), const(ln_b),
        ],
        out_specs=pl.BlockSpec((tile, D_MODEL), lambda t: (t, 0)),
        out_shape=jax.ShapeDtypeStruct((n, D_MODEL), F32),
        compiler_params=pltpu.CompilerParams(dimension_semantics=("parallel",),
                                             vmem_limit_bytes=VMEM_LIMIT),
        name="output",
    )(oaT, obT, x2, w_out, ln_g, ln_b)


def _prepare_weights(w_in, q_norm_g, w_uq, kv_norm_g, w_ukv):
    offs = np.concatenate([[0], np.cumsum(IN_SPLITS)])
    (wqa, wka, wva, wga, wiq, wik, wiw, wcq, wckv, wkr, wgb) = [
        w_in[:, int(offs[j]):int(offs[j + 1])] for j in range(len(IN_SPLITS))]
    d = w_in.shape[0]
    zeros = lambda c: jnp.zeros((d, c), w_in.dtype)
    widx = jnp.concatenate([wiq] + [wik] * (LANES // IDX_DIM) + [wiw, zeros(LANES - IDX_HEADS)], axis=1)
    wkpe = jnp.concatenate([zeros(B_NOPE_DIM), wkr, zeros(LANES - B_NOPE_DIM - B_ROPE_DIM)], axis=1)
    wgT = jnp.concatenate([wga, wgb], axis=1).T

    qk = B_NOPE_DIM + B_ROPE_DIM
    wuq = w_uq.reshape(Q_LORA, B_HEADS, qk)
    wuq = jnp.pad(wuq, ((0, 0), (0, 0), (0, LANES - qk))).reshape(Q_LORA, B_HEADS * LANES)
    wukv = w_ukv.reshape(KV_LORA, B_HEADS, B_NOPE_DIM + B_V_DIM)
    wukvk = jnp.pad(wukv[:, :, :B_NOPE_DIM], ((0, 0), (0, 0), (0, LANES - B_NOPE_DIM)))
    wukvk = wukvk.reshape(KV_LORA, B_HEADS * LANES)
    wukvvT = wukv[:, :, B_NOPE_DIM:].reshape(KV_LORA, B_WIDTH).T

    bf = lambda a: a.astype(BF16)
    order = (np.arange(A_HEADS)[:, None] * A_HEAD_DIM + _head64_order()[None, :]).reshape(-1)
    return [bf(wqa[:, order]), bf(wka[:, order]), widx.astype(F32), bf(wkpe), bf(wcq), bf(wckv),
            bf(wva.T), bf(wgT),
            q_norm_g.reshape(1, Q_LORA).astype(F32), kv_norm_g.reshape(1, KV_LORA).astype(F32),
            bf(wuq), bf(wukvk), bf(wukvvT)]


def kernel(x, positions, w_in, q_norm_g, w_uq, kv_norm_g, w_ukv, w_out, ln_g, ln_b):
    batch, seq, d_model = x.shape
    depth = w_in.shape[0]
    alpha = (2 * depth) ** 0.25
    assert d_model == D_MODEL and seq % ROW_TILE == 0 and ROW_TILE % CHUNK == 0
    n = batch * seq
    pos_col = positions.reshape(n, 1).astype(I32)
    x2 = x.reshape(n, d_model)
    for l in range(depth):
        weights = _prepare_weights(w_in[l], q_norm_g[l], w_uq[l], kv_norm_g[l], w_ukv[l])
        qa, ka, iq, ik, qb, kb, vaT, vbT, gT, iwT = _projection(x2, pos_col, weights)
        oaT = _dsa_attention(iq, iwT, ik, qa, ka, vaT, gT, batch, seq)
        obT = _mla_attention(qb, kb, vbT, gT, batch, seq)
        x2 = _output(oaT, obT, x2, w_out[l].astype(BF16),
                     ln_g[l].reshape(1, d_model), ln_b[l].reshape(1, d_model), alpha)
    return x2.reshape(batch, seq, d_model)
```
